```python
import jax, jax.numpy as jnp
from jax import lax
import numpy as np

D_MODEL = 1024
BATCH = 2
SEQ = 8192
DEPTH = 2
DEC_BATCH = 32
DEC_SEQ = 4
PAST_LEN = 8192
PAGE_SIZE = 128

D_CONV = D_MODEL
CONV_W = 31
CONV_LN_EPS = 1e-5
SB_HEADS = 16
SB_HEAD_DIM = D_MODEL // SB_HEADS
D_SB = SB_HEADS * SB_HEAD_DIM
SB_SCALE = SB_HEAD_DIM ** -0.5
SB_BIAS_INIT = -7.0
Q_BLOCK = 128
RW_HEAD_DIM = 64
RW_HEADS = D_MODEL // RW_HEAD_DIM
D_RW = RW_HEADS * RW_HEAD_DIM
DECAY_RANK = 64
ICLR_RANK = 64
LNX_EPS = 64e-5
KK_EPS = 1e-12
N_SHIFT = 3 * D_RW + DECAY_RANK + ICLR_RANK
N_BRANCH = 3
NORM_EPS = 1e-6
SPLIT_SIZES = (D_CONV, D_CONV, D_CONV, D_SB, D_SB, D_SB, D_SB, N_SHIFT, D_RW, N_BRANCH * D_MODEL)
N_IN = 3 * D_CONV + 4 * D_SB + N_SHIFT + D_RW + N_BRANCH * D_MODEL

kernel_name = 'hybrid_conv_stickbreak_rwkv7_step'


def _split(u, sizes):
    idx = np.cumsum(sizes)[:-1].tolist()
    return jnp.split(u, idx, axis=-1)


def _rms_norm(x, g, eps=NORM_EPS):
    xf = x.astype(jnp.float32)
    y = xf * lax.rsqrt(jnp.mean(xf * xf, axis=-1, keepdims=True) + eps)
    return (y * g.astype(jnp.float32)).astype(x.dtype)


def _layer_norm(x, g, b, eps):
    xf = x.astype(jnp.float32)
    mu = jnp.mean(xf, axis=-1, keepdims=True)
    var = jnp.mean(jnp.square(xf - mu), axis=-1, keepdims=True)
    return ((xf - mu) * lax.rsqrt(var + eps) * g.astype(jnp.float32) + b.astype(jnp.float32)).astype(x.dtype)


def _conv_branch(c_val, c_gate, c_z, conv_prev, p):
    u = c_val * jax.nn.sigmoid(c_gate)
    u_ext = jnp.concatenate([conv_prev.astype(u.dtype), u], axis=1)
    c = lax.conv_general_dilated(u_ext, p['conv_w'][:, None, :].astype(u.dtype), window_strides=(1,),
                                 padding='VALID', dimension_numbers=('NWC', 'WIO', 'NWC'),
                                 feature_group_count=D_CONV)
    c = _layer_norm(c + p['conv_b'], p['conv_ln_g'], p['conv_ln_b'], CONV_LN_EPS)
    o = jax.nn.silu(c) * jax.nn.silu(c_z)
    return o @ p['w_pa'], u_ext[:, -(CONV_W - 1):]


def _sb_attend(q, k, v, q_pos, k_pos, bias):
    z = jnp.einsum('bqhd,bkhd->bhqk', q, k, preferred_element_type=jnp.float32) * SB_SCALE
    z = z + bias.astype(jnp.float32)[None, :, None, None]
    mask = (k_pos[None, :] < q_pos[:, None])[None, None]
    log_keep = jnp.where(mask, jax.nn.log_sigmoid(-z), 0.0)
    later = lax.cumsum(log_keep, axis=3, reverse=True) - log_keep
    a = jnp.where(mask, jnp.exp(jax.nn.log_sigmoid(z) + later), 0.0)
    return jnp.einsum('bhqk,bkhd->bqhd', a.astype(v.dtype), v)


def _sb_prompt(q, k, v, bias):
    t_len = q.shape[1]
    pos = jnp.arange(t_len)
    outs = []
    for i in range(t_len // Q_BLOCK):
        lo, hi = i * Q_BLOCK, (i + 1) * Q_BLOCK
        outs.append(_sb_attend(q[:, lo:hi], k[:, :hi], v[:, :hi], pos[lo:hi], pos[:hi], bias))
    return jnp.concatenate(outs, axis=1)


def _sb_with_past(past_k, past_v):
    def attend(q, k, v, bias):
        n_past = past_k.shape[1]
        k_all = jnp.concatenate([past_k.astype(k.dtype), k], axis=1)
        v_all = jnp.concatenate([past_v.astype(v.dtype), v], axis=1)
        q_pos = n_past + jnp.arange(q.shape[1])
        k_pos = jnp.arange(k_all.shape[1])
        return _sb_attend(q, k_all, v_all, q_pos, k_pos, bias)
    return attend


def _rwkv_branch(p_rw, z, shift_prev, wkv_prev, p):
    f32 = jnp.float32
    B, T, _ = p_rw.shape
    prev = jnp.concatenate([shift_prev[:, None, :].astype(p_rw.dtype), p_rw[:, :-1]], axis=1)
    s = (p_rw + (prev - p_rw) * p['mu_shift']).astype(f32)
    r, k, v, wd, ad = _split(s, (D_RW, D_RW, D_RW, DECAY_RANK, ICLR_RANK))
    log_decay = -jax.nn.softplus(-(p['w0'] + jnp.tanh(wd) @ p['w2'])) - 0.5
    decay = jnp.exp(-jnp.exp(log_decay))
    a = jax.nn.sigmoid(p['a0'] + ad @ p['a2'])
    heads = lambda t: t.reshape(B, T, RW_HEADS, RW_HEAD_DIM)
    kk = heads(k * p['k_k'])
    kk = kk * lax.rsqrt(jnp.sum(kk * kk, axis=-1, keepdims=True) + KK_EPS)
    k = k * (1.0 + (a - 1.0) * p['k_a'])
    r_h, k_h, v_h, w_h, a_h = heads(r), heads(k), heads(v), heads(decay), heads(a)

    def step(S, inp):
        r_t, k_t, v_t, w_t, kk_t, b_t = inp
        sa = jnp.einsum('bhij,bhj->bhi', S, kk_t)
        S = S * w_t[:, :, None, :] - sa[..., None] * b_t[:, :, None, :] + v_t[..., None] * k_t[:, :, None, :]
        return S, jnp.einsum('bhij,bhj->bhi', S, r_t)

    seq = tuple(jnp.moveaxis(t, 1, 0) for t in (r_h, k_h, v_h, w_h, kk, kk * a_h))
    S_T, ys = lax.scan(step, wkv_prev.astype(f32), seq)
    y = jnp.moveaxis(ys, 0, 1)
    mu = jnp.mean(y, axis=-1, keepdims=True)
    var = jnp.mean(jnp.square(y - mu), axis=-1, keepdims=True)
    yn = ((y - mu) * lax.rsqrt(var + LNX_EPS)).reshape(B, T, D_RW) * p['lnx_g'] + p['lnx_b']
    bonus = (jnp.sum(r_h * k_h * p['r_k'], axis=-1, keepdims=True) * v_h).reshape(B, T, D_RW)
    o = (yn + bonus) * jax.nn.silu(z.astype(f32))
    return (o.astype(z.dtype) @ p['w_pc']), S_T.astype(wkv_prev.dtype), p_rw[:, -1]


def _layer(x, conv_prev, shift_prev, wkv_prev, attend, p):
    B, T, _ = x.shape
    h = _rms_norm(x, p['norm_g'])
    u = h @ p['w_in']
    c_val, c_gate, c_z, q, k, v, sb_z, rw_p, rw_z, gates = _split(u, SPLIT_SIZES)
    y_a, conv_new = _conv_branch(c_val, c_gate, c_z, conv_prev, p)
    q = _rms_norm(q.reshape(B, T, SB_HEADS, SB_HEAD_DIM), p['q_norm_g'])
    k = _rms_norm(k.reshape(B, T, SB_HEADS, SB_HEAD_DIM), p['k_norm_g'])
    v = v.reshape(B, T, SB_HEADS, SB_HEAD_DIM)
    o = attend(q, k, v, p['sb_bias']).reshape(B, T, D_SB)
    y_b = (o * jax.nn.silu(sb_z)) @ p['w_pb']
    y_c, wkv_new, shift_new = _rwkv_branch(rw_p, rw_z, shift_prev, wkv_prev, p)
    g = jax.nn.sigmoid(gates.astype(jnp.float32)).reshape(B, T, N_BRANCH, D_MODEL)
    merged = g[:, :, 0] * y_a + g[:, :, 1] * y_b + g[:, :, 2] * y_c
    x = x + (merged.astype(x.dtype) @ p['w_out']).astype(x.dtype)
    return x, k, v, wkv_new, shift_new, conv_new


def setup_inputs(seed: int = 0) -> dict:
    key = jax.random.key(seed)
    ks = jax.random.split(key, 32)
    f32 = jnp.float32
    n_pages = PAST_LEN // PAGE_SIZE
    n_used = DEC_BATCH * n_pages
    n_pool = (5 * n_used + 3) // 4

    def nrm(k, shape, scale):
        return jax.random.normal(k, shape, f32) * scale

    def gain(k, shape):
        return 1.0 + 0.02 * jax.random.normal(k, shape, f32)

    page_table = jax.random.permutation(ks[7], n_pool)[:n_used].reshape(DEC_BATCH, n_pages).astype(jnp.int32)
    return {
        'x_prompt': nrm(ks[0], (BATCH, SEQ, D_MODEL), 1.0),
        'x_sample': nrm(ks[1], (DEC_BATCH, DEC_SEQ, D_MODEL), 1.0),
        'cache_k': nrm(ks[2], (n_pool, DEPTH, PAGE_SIZE, SB_HEADS, SB_HEAD_DIM), 1.0),
        'cache_v': nrm(ks[3], (n_pool, DEPTH, PAGE_SIZE, SB_HEADS, SB_HEAD_DIM), 1.0),
        'state_wkv': nrm(ks[4], (DEC_BATCH, DEPTH, RW_HEADS, RW_HEAD_DIM, RW_HEAD_DIM), 0.3),
        'state_shift': nrm(ks[5], (DEC_BATCH, DEPTH, N_SHIFT), 1.0),
        'state_conv': nrm(ks[6], (DEC_BATCH, DEPTH, CONV_W - 1, D_CONV), 0.5),
        'page_table': page_table,
        'norm_g': gain(ks[8], (DEPTH, D_MODEL)),
        'w_in': nrm(ks[9], (DEPTH, D_MODEL, N_IN), D_MODEL ** -0.5),
        'conv_w': nrm(ks[10], (DEPTH, CONV_W, D_CONV), CONV_W ** -0.5),
        'conv_b': nrm(ks[11], (DEPTH, D_CONV), 0.02),
        'conv_ln_g': gain(ks[12], (DEPTH, D_CONV)),
        'conv_ln_b': nrm(ks[13], (DEPTH, D_CONV), 0.02),
        'w_pa': nrm(ks[14], (DEPTH, D_CONV, D_MODEL), D_CONV ** -0.5),
        'q_norm_g': gain(ks[15], (DEPTH, SB_HEAD_DIM)),
        'k_norm_g': gain(ks[16], (DEPTH, SB_HEAD_DIM)),
        'sb_bias': SB_BIAS_INIT + 0.3 * jax.random.normal(ks[30], (DEPTH, SB_HEADS), f32),
        'w_pb': nrm(ks[17], (DEPTH, D_SB, D_MODEL), D_SB ** -0.5),
        'mu_shift': jax.random.uniform(ks[18], (DEPTH, N_SHIFT), f32),
        'w0': jax.random.uniform(ks[19], (DEPTH, D_RW), f32, -4.0, 1.0),
        'w2': nrm(ks[20], (DEPTH, DECAY_RANK, D_RW), 0.5 * DECAY_RANK ** -0.5),
        'a0': nrm(ks[21], (DEPTH, D_RW), 0.1),
        'a2': nrm(ks[22], (DEPTH, ICLR_RANK, D_RW), 0.5 * ICLR_RANK ** -0.5),
        'k_k': 0.85 + 0.02 * jax.random.normal(ks[23], (DEPTH, D_RW), f32),
        'k_a': gain(ks[24], (DEPTH, D_RW)),
        'r_k': nrm(ks[25], (DEPTH, RW_HEADS, RW_HEAD_DIM), 0.1),
        'lnx_g': gain(ks[26], (DEPTH, D_RW)),
        'lnx_b': nrm(ks[27], (DEPTH, D_RW), 0.02),
        'w_pc': nrm(ks[28], (DEPTH, D_RW, D_MODEL), D_RW ** -0.5),
        'w_out': nrm(ks[29], (DEPTH, D_MODEL, D_MODEL), D_MODEL ** -0.5),
    }


def reference(x_prompt, x_sample, cache_k, cache_v, state_wkv, state_shift, state_conv, page_table,
              norm_g, w_in, conv_w, conv_b, conv_ln_g, conv_ln_b, w_pa, q_norm_g, k_norm_g, sb_bias, w_pb,
              mu_shift, w0, w2, a0, a2, k_k, k_a, r_k, lnx_g, lnx_b, w_pc, w_out):
    dt = x_prompt.dtype
    nb_p = x_prompt.shape[0]
    nb_s = x_sample.shape[0]
    n_past = page_table.shape[1] * PAGE_SIZE
    zero_conv = jnp.zeros((nb_p, CONV_W - 1, D_CONV), dt)
    zero_shift = jnp.zeros((nb_p, N_SHIFT), dt)
    zero_wkv = jnp.zeros((nb_p, RW_HEADS, RW_HEAD_DIM, RW_HEAD_DIM), dt)
    x_p, x_s = x_prompt, x_sample
    kp, vp, wp, shp, cp = [], [], [], [], []
    kq, vq, wq, shq, cq = [], [], [], [], []
    for l in range(DEPTH):
        p = {'norm_g': norm_g[l], 'w_in': w_in[l], 'conv_w': conv_w[l], 'conv_b': conv_b[l],
             'conv_ln_g': conv_ln_g[l], 'conv_ln_b': conv_ln_b[l], 'w_pa': w_pa[l],
             'q_norm_g': q_norm_g[l], 'k_norm_g': k_norm_g[l], 'sb_bias': sb_bias[l], 'w_pb': w_pb[l],
             'mu_shift': mu_shift[l], 'w0': w0[l], 'w2': w2[l], 'a0': a0[l], 'a2': a2[l],
             'k_k': k_k[l], 'k_a': k_a[l], 'r_k': r_k[l], 'lnx_g': lnx_g[l], 'lnx_b': lnx_b[l],
             'w_pc': w_pc[l], 'w_out': w_out[l]}
        x_p, k1, v1, s1, sh1, c1 = _layer(x_p, zero_conv, zero_shift, zero_wkv, _sb_prompt, p)
        kp.append(k1); vp.append(v1); wp.append(s1); shp.append(sh1); cp.append(c1)
        past_k = cache_k[page_table, l].reshape(nb_s, n_past, SB_HEADS, SB_HEAD_DIM)
        past_v = cache_v[page_table, l].reshape(nb_s, n_past, SB_HEADS, SB_HEAD_DIM)
        x_s, k2, v2, s2, sh2, c2 = _layer(x_s, state_conv[:, l], state_shift[:, l], state_wkv[:, l],
                                          _sb_with_past(past_k, past_v), p)
        kq.append(k2); vq.append(v2); wq.append(s2); shq.append(sh2); cq.append(c2)
    return (x_p, x_s,
            jnp.stack(kp, axis=1), jnp.stack(vp, axis=1), jnp.stack(wp, axis=1),
            jnp.stack(shp, axis=1), jnp.stack(cp, axis=1),
            jnp.stack(kq, axis=1), jnp.stack(vq, axis=1), jnp.stack(wq, axis=1),
            jnp.stack(shq, axis=1), jnp.stack(cq, axis=1))
```

```python
import functools

import jax
import jax.numpy as jnp
from jax import lax
from jax.experimental import pallas as pl
from jax.experimental.pallas import tpu as pltpu

F32 = jnp.float32
BF16 = jnp.bfloat16

D_MODEL = 1024
N_HEADS = 16
HEAD_DIM = 64
N_PAIRS = N_HEADS // 2
LANES = 128
SUBLANES = 8
CONV_W = 31
CONV_HALO = 32
LORA = 64
PAGE = 128
NORM_EPS = 1e-6
CONV_LN_EPS = 1e-5
LNX_EPS = 64e-5
KK_EPS = 1e-12
SB_SCALE = HEAD_DIM ** -0.5

C_VAL, C_GATE, C_Z, C_Q, C_K, C_V, C_SBZ, C_R, C_RK, C_RV, C_RZ, C_G0, C_G1, C_G2 = range(14)
N_MAIN = 14 * D_MODEL

VMEM_LIMIT = 48 * 1024 * 1024


def _cparams(*sem):
    return pltpu.CompilerParams(dimension_semantics=sem, vmem_limit_bytes=VMEM_LIMIT)


def _sigmoid(x):
    return 1.0 / (1.0 + jnp.exp(-x))


def _silu(x):
    return x * _sigmoid(x)


def _softplus(x):
    return jnp.maximum(x, 0.0) + jnp.log(1.0 + jnp.exp(-jnp.abs(x)))


def _split_bf16(x):
    hi = x.astype(BF16)
    lo = (x - hi.astype(F32)).astype(BF16)
    return hi, lo


def _seg_sum(x, seg_ref):
    hi, lo = _split_bf16(x)
    seg = seg_ref[...]
    return (jnp.dot(hi, seg, preferred_element_type=F32)
            + jnp.dot(lo, seg, preferred_element_type=F32))


def _row_tile(m, pref):
    t = min(m, pref)
    assert m % t == 0
    return t


def _in_proj_body(x_ref, g_ref, w_ref, o_ref, h_ref):
    @pl.when(pl.program_id(1) == 0)
    def _():
        x = x_ref[...]
        ms = jnp.mean(x * x, axis=-1, keepdims=True)
        h_ref[...] = (x * lax.rsqrt(ms + NORM_EPS) * g_ref[...]).astype(BF16)

    o_ref[...] = jnp.dot(h_ref[...], w_ref[...], preferred_element_type=F32)


def _in_proj(x, g, w, tn):
    m, n = x.shape[0], w.shape[1]
    tm = _row_tile(m, 1024)
    return pl.pallas_call(
        _in_proj_body,
        grid=(m // tm, n // tn),
        in_specs=[pl.BlockSpec((tm, D_MODEL), lambda i, j: (i, 0)),
                  pl.BlockSpec((1, D_MODEL), lambda i, j: (0, 0)),
                  pl.BlockSpec((D_MODEL, tn), lambda i, j: (0, j))],
        out_specs=pl.BlockSpec((tm, tn), lambda i, j: (i, j)),
        out_shape=jax.ShapeDtypeStruct((m, n), F32),
        scratch_shapes=[pltpu.VMEM((tm, D_MODEL), BF16)],
        compiler_params=_cparams("parallel", "arbitrary"),
        name="in_proj",
    )(x, g, w)


def _pre_body(cv_ref, cg_ref, q_ref, k_ref, v_ref, qg_ref, kg_ref, seg_ref,
              glu_ref, qb_ref, kf_ref, kb_ref, vb_ref):
    glu_ref[...] = cv_ref[...] * _sigmoid(cg_ref[...])
    q = q_ref[...]
    k = k_ref[...]
    q_ms = _seg_sum(q * q, seg_ref) * (1.0 / HEAD_DIM)
    k_ms = _seg_sum(k * k, seg_ref) * (1.0 / HEAD_DIM)
    qn = q * lax.rsqrt(q_ms + NORM_EPS) * qg_ref[...]
    kn = k * lax.rsqrt(k_ms + NORM_EPS) * kg_ref[...]
    qb_ref[...] = (qn * SB_SCALE).astype(BF16)
    kf_ref[...] = kn
    kb_ref[...] = kn.astype(BF16)
    vb_ref[...] = v_ref[...].astype(BF16)


def _pre(u, qg, kg, seg):
    m = u.shape[0]
    tm = _row_tile(m, 512)
    col = lambda c: pl.BlockSpec((tm, D_MODEL), lambda i, c=c: (i, c))
    vec = pl.BlockSpec((1, D_MODEL), lambda i: (0, 0))
    out = pl.BlockSpec((tm, D_MODEL), lambda i: (i, 0))
    sds = lambda dt: jax.ShapeDtypeStruct((m, D_MODEL), dt)
    return pl.pallas_call(
        _pre_body,
        grid=(m // tm,),
        in_specs=[col(C_VAL), col(C_GATE), col(C_Q), col(C_K), col(C_V), vec, vec,
                  pl.BlockSpec((D_MODEL, D_MODEL), lambda i: (0, 0))],
        out_specs=[out] * 5,
        out_shape=[sds(F32), sds(BF16), sds(F32), sds(BF16), sds(BF16)],
        compiler_params=_cparams("parallel"),
        name="pre",
    )(u, u, u, u, u, qg, kg, seg)


def _conv_body(cur_ref, halo_ref, cz_ref, cw_ref, cb_ref, lg_ref, lb_ref, o_ref, ext_ref):
    tt = cur_ref.shape[1]
    ext_ref[0:CONV_HALO, :] = halo_ref[0]
    ext_ref[CONV_HALO:CONV_HALO + tt, :] = cur_ref[0]
    first = CONV_HALO - (CONV_W - 1)
    acc = jnp.zeros((tt, D_MODEL), F32)
    for w in range(CONV_W):
        acc = acc + cw_ref[w:w + 1, :] * ext_ref[first + w:first + w + tt, :]
    c = acc + cb_ref[...]
    mu = jnp.mean(c, axis=-1, keepdims=True)
    d = c - mu
    var = jnp.mean(d * d, axis=-1, keepdims=True)
    cn = d * lax.rsqrt(var + CONV_LN_EPS) * lg_ref[...] + lb_ref[...]
    o_ref[0] = (_silu(cn) * _silu(cz_ref[0])).astype(BF16)


def _conv(glu, ext, u, cw, cb, lg, lb):
    b, t, _ = glu.shape
    tt = _row_tile(t, 128)
    assert tt % CONV_HALO == 0 or t == tt
    hb = max(tt // CONV_HALO, 1)
    vec = pl.BlockSpec((1, D_MODEL), lambda bi, i: (0, 0))
    return pl.pallas_call(
        _conv_body,
        grid=(b, t // tt),
        in_specs=[pl.BlockSpec((1, tt, D_MODEL), lambda bi, i: (bi, i, 0)),
                  pl.BlockSpec((1, CONV_HALO, D_MODEL), lambda bi, i: (bi, i * hb, 0)),
                  pl.BlockSpec((1, tt, D_MODEL), lambda bi, i: (bi, i, C_Z)),
                  pl.BlockSpec((CONV_HALO, D_MODEL), lambda bi, i: (0, 0)),
                  vec, vec, vec],
        out_specs=pl.BlockSpec((1, tt, D_MODEL), lambda bi, i: (bi, i, 0)),
        out_shape=jax.ShapeDtypeStruct((b, t, D_MODEL), BF16),
        scratch_shapes=[pltpu.VMEM((CONV_HALO + tt, D_MODEL), F32)],
        compiler_params=_cparams("parallel", "parallel"),
        name="conv",
    )(glu, ext, u, cw, cb, lg, lb)


def _sb_prompt_body(bias_ref, q_ref, k_ref, v_ref, z_ref, tri_ref, o_ref, *, tq, tk):
    p = pl.program_id(1)
    i = pl.program_id(2)
    q2 = q_ref[0]
    lane_q = lax.broadcasted_iota(jnp.int32, (tq, LANES), 1)
    lane_k = lax.broadcasted_iota(jnp.int32, (tk, LANES), 1)
    zero_q = jnp.zeros((tq, LANES), BF16)
    zero_k = jnp.zeros((tk, LANES), BF16)
    q_heads = (jnp.where(lane_q < HEAD_DIM, q2, zero_q), jnp.where(lane_q >= HEAD_DIM, q2, zero_q))
    biases = (bias_ref[2 * p], bias_ref[2 * p + 1])
    tri = tri_ref[...]
    row = lax.broadcasted_iota(jnp.int32, (tq, tk), 0)
    colk = lax.broadcasted_iota(jnp.int32, (tq, tk), 1)

    def block(j, carry, masked):
        start = pl.multiple_of(j * tk, tk)
        kb = k_ref[0, pl.ds(start, tk), :]
        vb = v_ref[0, pl.ds(start, tk), :]
        v_heads = (jnp.where(lane_k < HEAD_DIM, vb, zero_k), jnp.where(lane_k >= HEAD_DIM, vb, zero_k))
        if masked:
            mask = (colk + j * tk) < (row + i * tq)
        out = []
        for h in range(2):
            c, acc = carry[2 * h], carry[2 * h + 1]
            s = lax.dot_general(q_heads[h], kb, (((1,), (1,)), ((), ())), preferred_element_type=F32)
            z = s + biases[h]
            sp = _softplus(z)
            if masked:
                sp = jnp.where(mask, sp, 0.0)
            hi, lo = _split_bf16(sp)
            later = (jnp.dot(hi, tri, preferred_element_type=F32)
                     + jnp.dot(lo, tri, preferred_element_type=F32))
            a = jnp.exp(z - sp - later - c)
            if masked:
                a = jnp.where(mask, a, 0.0)
            acc = acc + jnp.dot(a.astype(BF16), v_heads[h], preferred_element_type=F32)
            c = c + later[:, 0:1] + sp[:, 0:1]
            out += [c, acc]
        return tuple(out)

    zc = jnp.zeros((tq, 1), F32)
    za = jnp.zeros((tq, LANES), F32)
    carry = (zc, za, zc, za)
    n_diag = max(tq // tk, 1)
    first_diag = (i * tq) // tk
    for d in reversed(range(n_diag)):
        carry = block(first_diag + d, carry, True)

    def body(it, carry):
        return block(first_diag - 1 - it, carry, False)

    carry = lax.fori_loop(0, first_diag, body, carry)
    o = carry[1] + carry[3]
    o_ref[0] = (o * _silu(z_ref[0])).astype(BF16)


def _sb_prompt(qb, kb, vb, u, bias, tri, tq, tk):
    b, t, _ = qb.shape
    assert t % tq == 0 and tq % tk == 0
    return pl.pallas_call(
        functools.partial(_sb_prompt_body, tq=tq, tk=tk),
        grid=(b, N_PAIRS, t // tq),
        in_specs=[pl.BlockSpec(memory_space=pltpu.SMEM),
                  pl.BlockSpec((1, tq, LANES), lambda bi, p, i: (bi, i, p)),
                  pl.BlockSpec((1, t, LANES), lambda bi, p, i: (bi, 0, p)),
                  pl.BlockSpec((1, t, LANES), lambda bi, p, i: (bi, 0, p)),
                  pl.BlockSpec((1, tq, LANES), lambda bi, p, i: (bi, i, C_SBZ * N_PAIRS + p)),
                  pl.BlockSpec((tk, tk), lambda bi, p, i: (0, 0))],
        out_specs=pl.BlockSpec((1, tq, LANES), lambda bi, p, i: (bi, i, p)),
        out_shape=jax.ShapeDtypeStruct((b, t, D_MODEL), BF16),
        compiler_params=_cparams("parallel", "parallel", "arbitrary"),
        name="sb_prompt",
    )(bias, qb, kb, vb, u, tri)


def _sb_sample_body(pt_ref, qbd_ref, bias_ref, kn_ref, vn_ref, kp_ref, vp_ref, z_ref, trin_ref, trip_ref,
                    o_ref, acc_ref, c_ref, *, n_new):
    j = pl.program_id(1)
    qbd = qbd_ref[0]
    bias = bias_ref[...]
    n_cols = qbd.shape[1]

    def visit(keys, vals, tri, mask):
        z = jnp.dot(keys, qbd, preferred_element_type=F32) + bias
        sp = _softplus(z)
        if mask is not None:
            sp = jnp.where(mask, sp, 0.0)
        hi, lo = _split_bf16(sp)
        later = (jnp.dot(tri, hi, preferred_element_type=F32)
                 + jnp.dot(tri, lo, preferred_element_type=F32))
        a = jnp.exp(z - sp - later - c_ref[...])
        if mask is not None:
            a = jnp.where(mask, a, 0.0)
        acc_ref[...] += lax.dot_general(a.astype(BF16), vals, (((0,), (0,)), ((), ())),
                                        preferred_element_type=F32)
        c_ref[...] += later[0:1, :] + sp[0:1, :]

    @pl.when(j == 0)
    def _():
        acc_ref[...] = jnp.zeros_like(acc_ref)
        c_ref[...] = jnp.zeros_like(c_ref)
        n_pad = kn_ref.shape[1]
        key_i = lax.broadcasted_iota(jnp.int32, (n_pad, n_cols), 0)
        qry_i = lax.broadcasted_iota(jnp.int32, (n_pad, n_cols), 1) % n_new
        visit(kn_ref[0], vn_ref[0], trin_ref[...], key_i < qry_i)

    visit(kp_ref[0, 0].astype(BF16), vp_ref[0, 0].astype(BF16), trip_ref[...], None)

    @pl.when(j == pl.num_programs(1) - 1)
    def _():
        lane = lax.broadcasted_iota(jnp.int32, (n_new, LANES), 1)
        for m in range(N_PAIRS):
            tile = acc_ref[2 * n_new * m:2 * n_new * (m + 1), LANES * m:LANES * (m + 1)]
            o = jnp.where(lane < HEAD_DIM, tile[0:n_new], tile[n_new:2 * n_new])
            zz = z_ref[0, :, LANES * m:LANES * (m + 1)]
            o_ref[0, :, LANES * m:LANES * (m + 1)] = (o * _silu(zz)).astype(BF16)


def _sb_sample(page_table, qbd, bias_cols, k_new, v_new, cache_k, cache_v, layer, sbz, tri_new, tri_page):
    b, n_pages = page_table.shape
    n_new = sbz.shape[1]
    n_pad = k_new.shape[1]
    n_cols = qbd.shape[2]
    last = n_pages - 1
    grid_spec = pltpu.PrefetchScalarGridSpec(
        num_scalar_prefetch=1,
        grid=(b, n_pages),
        in_specs=[pl.BlockSpec((1, D_MODEL, n_cols), lambda bi, j, pt: (bi, 0, 0)),
                  pl.BlockSpec((1, n_cols), lambda bi, j, pt: (0, 0)),
                  pl.BlockSpec((1, n_pad, D_MODEL), lambda bi, j, pt: (bi, 0, 0)),
                  pl.BlockSpec((1, n_pad, D_MODEL), lambda bi, j, pt: (bi, 0, 0)),
                  pl.BlockSpec((1, 1, PAGE, D_MODEL), lambda bi, j, pt: (pt[bi, last - j], layer, 0, 0)),
                  pl.BlockSpec((1, 1, PAGE, D_MODEL), lambda bi, j, pt: (pt[bi, last - j], layer, 0, 0)),
                  pl.BlockSpec((1, n_new, D_MODEL), lambda bi, j, pt: (bi, 0, 0)),
                  pl.BlockSpec((n_pad, n_pad), lambda bi, j, pt: (0, 0)),
                  pl.BlockSpec((PAGE, PAGE), lambda bi, j, pt: (0, 0))],
        out_specs=pl.BlockSpec((1, n_new, D_MODEL), lambda bi, j, pt: (bi, 0, 0)),
        scratch_shapes=[pltpu.VMEM((n_cols, D_MODEL), F32), pltpu.VMEM((1, n_cols), F32)],
    )
    return pl.pallas_call(
        functools.partial(_sb_sample_body, n_new=n_new),
        grid_spec=grid_spec,
        out_shape=jax.ShapeDtypeStruct((b, n_new, D_MODEL), BF16),
        compiler_params=_cparams("parallel", "arbitrary"),
        name="sb_sample",
    )(page_table, qbd, bias_cols, k_new, v_new, cache_k, cache_v, sbz, tri_new, tri_page)


def _rw_prep_body(pr_ref, pk_ref, pv_ref, pl_ref, qr_ref, qk_ref, qv_ref, ql_ref,
                  mu_ref, mul_ref, w0_ref, w2_ref, a0_ref, a2_ref, kkg_ref, ka_ref, rk_ref, seg_ref,
                  r_ref, w_ref, k_ref, v_ref, kk_ref, b_ref, bonus_ref):
    def shift(cur_ref, prev_ref, mu):
        cur = cur_ref[...]
        return cur + (prev_ref[...] - cur) * mu

    r = shift(pr_ref, qr_ref, mu_ref[0:1, :])
    k = shift(pk_ref, qk_ref, mu_ref[1:2, :])
    v = shift(pv_ref, qv_ref, mu_ref[2:3, :])
    lo = shift(pl_ref, ql_ref, mul_ref[...])
    lane = lax.broadcasted_iota(jnp.int32, lo.shape, 1)
    wd = jnp.where(lane < LORA, jnp.tanh(lo), 0.0).astype(BF16)
    ad = jnp.where(lane >= LORA, lo, 0.0).astype(BF16)
    log_decay = -_softplus(-(w0_ref[...] + jnp.dot(wd, w2_ref[...], preferred_element_type=F32))) - 0.5
    decay = jnp.exp(-jnp.exp(log_decay))
    a = _sigmoid(a0_ref[...] + jnp.dot(ad, a2_ref[...], preferred_element_type=F32))
    kk = k * kkg_ref[...]
    kk = kk * lax.rsqrt(_seg_sum(kk * kk, seg_ref) + KK_EPS)
    k = k * (1.0 + (a - 1.0) * ka_ref[...])
    r_ref[...] = r
    w_ref[...] = decay
    k_ref[...] = k
    v_ref[...] = v
    kk_ref[...] = kk
    b_ref[...] = kk * a
    bonus_ref[...] = _seg_sum(r * k * rk_ref[...], seg_ref) * v


def _rw_prep(u, ul, prev, prev_l, mu3, mul, w0, w2p, a0, a2p, kkg, ka, rk, seg):
    m = u.shape[0]
    tm = _row_tile(m, 256)
    col = lambda c: pl.BlockSpec((tm, D_MODEL), lambda i, c=c: (i, c))
    vec = pl.BlockSpec((1, D_MODEL), lambda i: (0, 0))
    lora = pl.BlockSpec((tm, LANES), lambda i: (i, 0))
    lora_w = pl.BlockSpec((LANES, D_MODEL), lambda i: (0, 0))
    out = pl.BlockSpec((tm, D_MODEL), lambda i: (i, 0))
    return pl.pallas_call(
        _rw_prep_body,
        grid=(m // tm,),
        in_specs=[col(C_R), col(C_RK), col(C_RV), lora,
                  col(0), col(1), col(2), lora,
                  pl.BlockSpec((3, D_MODEL), lambda i: (0, 0)), pl.BlockSpec((1, LANES), lambda i: (0, 0)),
                  vec, lora_w, vec, lora_w, vec, vec, vec,
                  pl.BlockSpec((D_MODEL, D_MODEL), lambda i: (0, 0))],
        out_specs=[out] * 7,
        out_shape=[jax.ShapeDtypeStruct((m, D_MODEL), F32)] * 7,
        compiler_params=_cparams("parallel"),
        name="rw_prep",
    )(u, u, u, ul, prev, prev, prev, prev_l, mu3, mul, w0, w2p, a0, a2p, kkg, ka, rk, seg)


def _rw_scan_body(r_ref, w_ref, k_ref, v_ref, kk_ref, b_ref, s0_ref, y_ref, sT_ref,
                  s_ref, vt_ref, yt_ref, *, n_steps):
    c = pl.program_id(1)
    tc = r_ref.shape[1]

    @pl.when(c == 0)
    def _():
        s_ref[...] = s0_ref[0]

    yt_ref[...] = jnp.zeros_like(yt_ref)
    for p in range(N_PAIRS):
        vt_ref[p] = v_ref[0, :, LANES * p:LANES * (p + 1)].T

    lane_s = lax.broadcasted_iota(jnp.int32, (HEAD_DIM, LANES), 1)
    lane_t = lax.broadcasted_iota(jnp.int32, (HEAD_DIM, tc), 1)
    even = lane_s < HEAD_DIM

    def pair_sum(x):
        se = jnp.sum(jnp.where(even, x, 0.0), axis=1, keepdims=True)
        so = jnp.sum(jnp.where(even, 0.0, x), axis=1, keepdims=True)
        return se, so

    def group(g, carry):
        t0 = pl.multiple_of(g * SUBLANES, SUBLANES)
        for p in range(N_PAIRS):
            sl = slice(LANES * p, LANES * (p + 1))
            rows = [ref[0, pl.ds(t0, SUBLANES), sl] for ref in (kk_ref, w_ref, b_ref, k_ref, r_ref)]
            s = s_ref[p]
            for i in range(min(SUBLANES, n_steps)):
                kk_t, w_t, b_t, k_t, r_t = (x[i:i + 1, :] for x in rows)
                hot = lane_t == t0 + i
                sa_e, sa_o = pair_sum(s * kk_t)
                sa = jnp.where(even, sa_e, sa_o)
                v_e = jnp.sum(jnp.where(hot, vt_ref[p, 0:HEAD_DIM, :], 0.0), axis=1, keepdims=True)
                v_o = jnp.sum(jnp.where(hot, vt_ref[p, HEAD_DIM:LANES, :], 0.0), axis=1, keepdims=True)
                vc = jnp.where(even, v_e, v_o)
                s = s * w_t - sa * b_t + vc * k_t
                y_e, y_o = pair_sum(s * r_t)
                yt_ref[p, 0:HEAD_DIM, :] = jnp.where(hot, y_e, yt_ref[p, 0:HEAD_DIM, :])
                yt_ref[p, HEAD_DIM:LANES, :] = jnp.where(hot, y_o, yt_ref[p, HEAD_DIM:LANES, :])
            s_ref[p] = s
        return carry

    lax.fori_loop(0, -(-n_steps // SUBLANES), group, 0)

    for p in range(N_PAIRS):
        y_ref[0, :, LANES * p:LANES * (p + 1)] = yt_ref[p].T

    @pl.when(c == pl.num_programs(1) - 1)
    def _():
        sT_ref[0] = s_ref[...]


def _rw_scan(r, w, k, v, kk, b, s0, n_valid):
    bsz, t, _ = r.shape
    tc = LANES
    assert t % tc == 0 and (n_valid == tc or t == tc)
    assert n_valid % SUBLANES == 0 or n_valid < SUBLANES
    seq = pl.BlockSpec((1, tc, D_MODEL), lambda bi, c: (bi, c, 0))
    st = pl.BlockSpec((1, N_PAIRS, HEAD_DIM, LANES), lambda bi, c: (bi, 0, 0, 0))
    return pl.pallas_call(
        functools.partial(_rw_scan_body, n_steps=n_valid),
        grid=(bsz, t // tc),
        in_specs=[seq] * 6 + [st],
        out_specs=[seq, st],
        out_shape=[jax.ShapeDtypeStruct((bsz, t, D_MODEL), F32),
                   jax.ShapeDtypeStruct((bsz, N_PAIRS, HEAD_DIM, LANES), F32)],
        scratch_shapes=[pltpu.VMEM((N_PAIRS, HEAD_DIM, LANES), F32),
                        pltpu.VMEM((N_PAIRS, LANES, tc), F32),
                        pltpu.VMEM((N_PAIRS, LANES, tc), F32)],
        compiler_params=_cparams("parallel", "arbitrary"),
        name="rw_scan",
    )(r, w, k, v, kk, b, s0)


def _rw_post_body(y_ref, bonus_ref, z_ref, g_ref, b_ref, seg_ref, o_ref):
    y = y_ref[...]
    mu = _seg_sum(y, seg_ref) * (1.0 / HEAD_DIM)
    d = y - mu
    var = _seg_sum(d * d, seg_ref) * (1.0 / HEAD_DIM)
    yn = d * lax.rsqrt(var + LNX_EPS) * g_ref[...] + b_ref[...]
    o_ref[...] = ((yn + bonus_ref[...]) * _silu(z_ref[...])).astype(BF16)


def _rw_post(y, bonus, u, g, b, seg):
    m = y.shape[0]
    tm = _row_tile(m, 512)
    row = pl.BlockSpec((tm, D_MODEL), lambda i: (i, 0))
    vec = pl.BlockSpec((1, D_MODEL), lambda i: (0, 0))
    return pl.pallas_call(
        _rw_post_body,
        grid=(m // tm,),
        in_specs=[row, row, pl.BlockSpec((tm, D_MODEL), lambda i: (i, C_RZ)), vec, vec,
                  pl.BlockSpec((D_MODEL, D_MODEL), lambda i: (0, 0))],
        out_specs=row,
        out_shape=jax.ShapeDtypeStruct((m, D_MODEL), BF16),
        compiler_params=_cparams("parallel"),
        name="rw_post",
    )(y, bonus, u, g, b, seg)


def _merge_body(x_ref, oa_ref, ob_ref, oc_ref, g0_ref, g1_ref, g2_ref,
                wa_ref, wb_ref, wc_ref, wo_ref, o_ref):
    merged = (_sigmoid(g0_ref[...]) * jnp.dot(oa_ref[...], wa_ref[...], preferred_element_type=F32)
              + _sigmoid(g1_ref[...]) * jnp.dot(ob_ref[...], wb_ref[...], preferred_element_type=F32)
              + _sigmoid(g2_ref[...]) * jnp.dot(oc_ref[...], wc_ref[...], preferred_element_type=F32))
    o_ref[...] = x_ref[...] + jnp.dot(merged.astype(BF16), wo_ref[...], preferred_element_type=F32)


def _merge(x, oa, ob, oc, u, wa, wb, wc, wo):
    m = x.shape[0]
    tm = _row_tile(m, 512)
    row = pl.BlockSpec((tm, D_MODEL), lambda i: (i, 0))
    col = lambda c: pl.BlockSpec((tm, D_MODEL), lambda i, c=c: (i, c))
    wsp = pl.BlockSpec((D_MODEL, D_MODEL), lambda i: (0, 0))
    return pl.pallas_call(
        _merge_body,
        grid=(m // tm,),
        in_specs=[row, row, row, row, col(C_G0), col(C_G1), col(C_G2), wsp, wsp, wsp, wsp],
        out_specs=row,
        out_shape=jax.ShapeDtypeStruct((m, D_MODEL), F32),
        compiler_params=_cparams("parallel"),
        name="merge",
    )(x, oa, ob, oc, u, u, u, wa, wb, wc, wo)


def _later_matrix(n, lhs):
    i = jnp.arange(n)
    m = (i[:, None] > i[None, :]) if not lhs else (i[None, :] > i[:, None])
    return m.astype(BF16)


def _pack_pairs(s):
    b = s.shape[0]
    return s.reshape(b, N_PAIRS, 2, HEAD_DIM, HEAD_DIM).transpose(0, 1, 3, 2, 4).reshape(b, N_PAIRS, HEAD_DIM, LANES)


def _unpack_pairs(s):
    b = s.shape[0]
    return s.reshape(b, N_PAIRS, HEAD_DIM, 2, HEAD_DIM).transpose(0, 1, 3, 2, 4).reshape(b, N_HEADS, HEAD_DIM, HEAD_DIM)


def _layer(x, conv_prev, shift_prev, wkv_prev, lw, attend):
    b, t, _ = x.shape
    m = b * t
    x2 = x.reshape(m, D_MODEL)
    u = _in_proj(x2, lw['norm_g'], lw['w_main'], 1024)
    ul = _in_proj(x2, lw['norm_g'], lw['w_lora'], LANES)

    glu, qb, kf, kb, vb = _pre(u, lw['q_g'], lw['k_g'], lw['seg'])
    u3 = u.reshape(b, t, N_MAIN)

    glu3 = glu.reshape(b, t, D_MODEL)
    hist = jnp.concatenate([jnp.zeros((b, CONV_HALO - (CONV_W - 1), D_MODEL), F32), conv_prev], axis=1)
    ext = jnp.concatenate([hist, glu3], axis=1)
    oa = _conv(glu3, ext, u3, lw['conv_w'], lw['conv_b'], lw['conv_ln_g'], lw['conv_ln_b'])
    conv_new = ext[:, -(CONV_W - 1):]

    ob = attend(qb.reshape(b, t, D_MODEL), kb.reshape(b, t, D_MODEL), vb.reshape(b, t, D_MODEL), u3)
    v_out = u3[:, :, C_V * D_MODEL:(C_V + 1) * D_MODEL]

    p_main = u3[:, :, C_R * D_MODEL:(C_RV + 1) * D_MODEL]
    ul3 = ul.reshape(b, t, LANES)
    prev = jnp.concatenate([shift_prev[:, None, :3 * D_MODEL], p_main[:, :-1]], axis=1).reshape(m, 3 * D_MODEL)
    prev_l = jnp.concatenate([shift_prev[:, None, 3 * D_MODEL:], ul3[:, :-1]], axis=1).reshape(m, LANES)
    r, w, k, v, kk, bb, bonus = _rw_prep(u, ul, prev, prev_l, lw['mu3'], lw['mul'], lw['w0'], lw['w2p'],
                                         lw['a0'], lw['a2p'], lw['k_k'], lw['k_a'], lw['r_k'], lw['seg'])
    t_pad = -(-t // LANES) * LANES
    seqs = [a.reshape(b, t, D_MODEL) for a in (r, w, k, v, kk, bb)]
    if t_pad != t:
        seqs = [jnp.pad(a, ((0, 0), (0, t_pad - t), (0, 0))) for a in seqs]
    y, s_new = _rw_scan(*seqs, _pack_pairs(wkv_prev), min(t, LANES))
    y = y[:, :t].reshape(m, D_MODEL)
    oc = _rw_post(y, bonus, u, lw['lnx_g'], lw['lnx_b'], lw['seg'])
    shift_new = jnp.concatenate([p_main[:, -1], ul3[:, -1]], axis=-1)

    x_new = _merge(x2, oa.reshape(m, D_MODEL), ob.reshape(m, D_MODEL), oc, u,
                   lw['w_pa'], lw['w_pb'], lw['w_pc'], lw['w_out'])
    return (x_new.reshape(b, t, D_MODEL), kf.reshape(b, t, N_HEADS, HEAD_DIM),
            v_out.reshape(b, t, N_HEADS, HEAD_DIM), _unpack_pairs(s_new), shift_new, conv_new)


def kernel(x_prompt, x_sample, cache_k, cache_v, state_wkv, state_shift, state_conv, page_table, norm_g, w_in,
           conv_w, conv_b, conv_ln_g, conv_ln_b, w_pa, q_norm_g, k_norm_g, sb_bias, w_pb, mu_shift, w0, w2, a0,
           a2, k_k, k_a, r_k, lnx_g, lnx_b, w_pc, w_out):
    depth = w_in.shape[0]
    nb_p, t_p, _ = x_prompt.shape
    nb_s, t_s, _ = x_sample.shape
    n_pool = cache_k.shape[0]
    lora0 = 10 * D_MODEL
    seg = jnp.kron(jnp.eye(N_HEADS, dtype=F32), jnp.ones((HEAD_DIM, HEAD_DIM), F32)).astype(BF16)
    vec = lambda a: a.reshape(1, -1)
    cache_k4 = cache_k.reshape(n_pool, depth, PAGE, D_MODEL)
    cache_v4 = cache_v.reshape(n_pool, depth, PAGE, D_MODEL)
    n_pad = 8
    assert t_s <= n_pad
    tri_prompt = _later_matrix(256, lhs=False)
    tri_new = _later_matrix(n_pad, lhs=True)
    tri_page = _later_matrix(PAGE, lhs=True)
    qi_head = jnp.arange(N_HEADS * t_s) // t_s
    row_head = jnp.arange(D_MODEL) // HEAD_DIM

    x_p, x_s = x_prompt, x_sample
    outs_p = [[] for _ in range(5)]
    outs_s = [[] for _ in range(5)]
    for l in range(depth):
        wl = w_in[l]
        zpad = jnp.zeros((LORA, D_MODEL), F32)
        lw = {
            'norm_g': vec(norm_g[l]),
            'w_main': jnp.concatenate([wl[:, :lora0], wl[:, lora0 + 2 * LORA:]], axis=1).astype(BF16),
            'w_lora': wl[:, lora0:lora0 + 2 * LORA].astype(BF16),
            'q_g': vec(jnp.tile(q_norm_g[l], N_HEADS)), 'k_g': vec(jnp.tile(k_norm_g[l], N_HEADS)),
            'seg': seg,
            'conv_w': jnp.concatenate([conv_w[l], jnp.zeros((CONV_HALO - CONV_W, D_MODEL), F32)], axis=0),
            'conv_b': vec(conv_b[l]), 'conv_ln_g': vec(conv_ln_g[l]), 'conv_ln_b': vec(conv_ln_b[l]),
            'mu3': mu_shift[l, :3 * D_MODEL].reshape(3, D_MODEL), 'mul': vec(mu_shift[l, 3 * D_MODEL:]),
            'w0': vec(w0[l]), 'w2p': jnp.concatenate([w2[l], zpad], axis=0).astype(BF16),
            'a0': vec(a0[l]), 'a2p': jnp.concatenate([zpad, a2[l]], axis=0).astype(BF16),
            'k_k': vec(k_k[l]), 'k_a': vec(k_a[l]), 'r_k': vec(r_k[l]),
            'lnx_g': vec(lnx_g[l]), 'lnx_b': vec(lnx_b[l]),
            'w_pa': w_pa[l].astype(BF16), 'w_pb': w_pb[l].astype(BF16), 'w_pc': w_pc[l].astype(BF16),
            'w_out': w_out[l].astype(BF16),
        }
        bias_l = sb_bias[l]

        def attend_prompt(qb, kb, vb, u3):
            return _sb_prompt(qb, kb, vb, u3, bias_l, tri_prompt, 256, 256)

        def attend_sample(qb, kb, vb, u3):
            qt = jnp.swapaxes(qb, 1, 2)
            qbd = jnp.tile(qt, (1, 1, N_HEADS))
            qbd = jnp.where(row_head[:, None] == qi_head[None, :], qbd, jnp.zeros_like(qbd))
            pad = ((0, 0), (0, n_pad - t_s), (0, 0))
            sbz = u3[:, :, C_SBZ * D_MODEL:(C_SBZ + 1) * D_MODEL]
            return _sb_sample(page_table, qbd, vec(jnp.repeat(bias_l, t_s)), jnp.pad(kb, pad), jnp.pad(vb, pad),
                              cache_k4, cache_v4, l, sbz, tri_new, tri_page)

        res_p = _layer(x_p, jnp.zeros((nb_p, CONV_W - 1, D_MODEL), F32), jnp.zeros((nb_p, state_shift.shape[2]), F32),
                       jnp.zeros((nb_p,) + state_wkv.shape[2:], F32), lw, attend_prompt)
        res_s = _layer(x_s, state_conv[:, l], state_shift[:, l], state_wkv[:, l], lw, attend_sample)
        x_p, x_s = res_p[0], res_s[0]
        for i in range(5):
            outs_p[i].append(res_p[i + 1])
            outs_s[i].append(res_s[i + 1])
    stack = lambda xs: jnp.stack(xs, axis=1)
    return (x_p, x_s) + tuple(stack(o) for o in outs_p) + tuple(stack(o) for o in outs_s)
```

```python
import functools

import jax
import jax.numpy as jnp
from jax import lax
from jax.experimental import pallas as pl
from jax.experimental.pallas import tpu as pltpu

F32 = jnp.float32
BF16 = jnp.bfloat16

D_MODEL = 1024
N_HEADS = 16
HEAD_DIM = 64
N_PAIRS = N_HEADS // 2
LANES = 128
SUBLANES = 8
CONV_W = 31
CONV_HALO = 32
LORA = 64
PAGE = 128
NORM_EPS = 1e-6
CONV_LN_EPS = 1e-5
LNX_EPS = 64e-5
KK_EPS = 1e-12
SB_SCALE = HEAD_DIM ** -0.5
LOG2E = 1.4426950408889634
_NT = (((1,), (1,)), ((), ()))
_TN = (((0,), (0,)), ((), ()))
_NN = (((1,), (0,)), ((), ()))

C_VAL, C_GATE, C_Z, C_Q, C_K, C_V, C_SBZ, C_R, C_RK, C_RV, C_RZ, C_G0, C_G1, C_G2 = range(14)
N_MAIN = 14 * D_MODEL

VMEM_LIMIT = 48 * 1024 * 1024
SB_TQ, SB_TK, SB_SPLIT = 256, 256, 2


def _cparams(*sem):
    return pltpu.CompilerParams(dimension_semantics=sem, vmem_limit_bytes=VMEM_LIMIT)


def _sigmoid(x):
    return 1.0 / (1.0 + jnp.exp(-x))


def _silu(x):
    return x * _sigmoid(x)


def _softplus(x):
    return jnp.maximum(x, 0.0) + jnp.log(1.0 + jnp.exp(-jnp.abs(x)))


def _split_bf16(x):
    hi = x.astype(BF16)
    lo = (x - hi.astype(F32)).astype(BF16)
    return hi, lo


def _seg_sum(x, seg_ref):
    hi, lo = _split_bf16(x)
    seg = seg_ref[...]
    return (jnp.dot(hi, seg, preferred_element_type=F32)
            + jnp.dot(lo, seg, preferred_element_type=F32))


def _row_tile(m, pref):
    t = min(m, pref)
    assert m % t == 0
    return t


def _in_proj_body(x_ref, g_ref, w_ref, o_ref, h_ref):
    @pl.when(pl.program_id(1) == 0)
    def _():
        x = x_ref[...]
        ms = jnp.mean(x * x, axis=-1, keepdims=True)
        h_ref[...] = (x * lax.rsqrt(ms + NORM_EPS) * g_ref[...]).astype(BF16)

    o_ref[...] = jnp.dot(h_ref[...], w_ref[...], preferred_element_type=F32)


def _in_proj(x, g, w, tn):
    m, n = x.shape[0], w.shape[1]
    tm = _row_tile(m, 1024)
    return pl.pallas_call(
        _in_proj_body,
        grid=(m // tm, n // tn),
        in_specs=[pl.BlockSpec((tm, D_MODEL), lambda i, j: (i, 0)),
                  pl.BlockSpec((1, D_MODEL), lambda i, j: (0, 0)),
                  pl.BlockSpec((D_MODEL, tn), lambda i, j: (0, j))],
        out_specs=pl.BlockSpec((tm, tn), lambda i, j: (i, j)),
        out_shape=jax.ShapeDtypeStruct((m, n), F32),
        scratch_shapes=[pltpu.VMEM((tm, D_MODEL), BF16)],
        compiler_params=_cparams("parallel", "arbitrary"),
        name="in_proj",
    )(x, g, w)


def _pre_body(cv_ref, cg_ref, q_ref, k_ref, v_ref, qg_ref, kg_ref, seg_ref,
              glu_ref, qb_ref, kf_ref, kb_ref, vb_ref):
    glu_ref[...] = cv_ref[...] * _sigmoid(cg_ref[...])
    q = q_ref[...]
    k = k_ref[...]
    q_ms = _seg_sum(q * q, seg_ref) * (1.0 / HEAD_DIM)
    k_ms = _seg_sum(k * k, seg_ref) * (1.0 / HEAD_DIM)
    qn = q * lax.rsqrt(q_ms + NORM_EPS) * qg_ref[...]
    kn = k * lax.rsqrt(k_ms + NORM_EPS) * kg_ref[...]
    qb_ref[...] = (qn * (SB_SCALE * LOG2E)).astype(BF16)
    kf_ref[...] = kn
    kb_ref[...] = kn.astype(BF16)
    vb_ref[...] = v_ref[...].astype(BF16)


def _pre(u, qg, kg, seg):
    m = u.shape[0]
    tm = _row_tile(m, 512)
    col = lambda c: pl.BlockSpec((tm, D_MODEL), lambda i, c=c: (i, c))
    vec = pl.BlockSpec((1, D_MODEL), lambda i: (0, 0))
    out = pl.BlockSpec((tm, D_MODEL), lambda i: (i, 0))
    sds = lambda dt: jax.ShapeDtypeStruct((m, D_MODEL), dt)
    return pl.pallas_call(
        _pre_body,
        grid=(m // tm,),
        in_specs=[col(C_VAL), col(C_GATE), col(C_Q), col(C_K), col(C_V), vec, vec,
                  pl.BlockSpec((D_MODEL, D_MODEL), lambda i: (0, 0))],
        out_specs=[out] * 5,
        out_shape=[sds(F32), sds(BF16), sds(F32), sds(BF16), sds(BF16)],
        compiler_params=_cparams("parallel"),
        name="pre",
    )(u, u, u, u, u, qg, kg, seg)


def _conv_body(cur_ref, halo_ref, cz_ref, cw_ref, cb_ref, lg_ref, lb_ref, o_ref, ext_ref):
    tt = cur_ref.shape[1]
    ext_ref[0:CONV_HALO, :] = halo_ref[0]
    ext_ref[CONV_HALO:CONV_HALO + tt, :] = cur_ref[0]
    first = CONV_HALO - (CONV_W - 1)
    acc = jnp.zeros((tt, D_MODEL), F32)
    for w in range(CONV_W):
        acc = acc + cw_ref[w:w + 1, :] * ext_ref[first + w:first + w + tt, :]
    c = acc + cb_ref[...]
    mu = jnp.mean(c, axis=-1, keepdims=True)
    d = c - mu
    var = jnp.mean(d * d, axis=-1, keepdims=True)
    cn = d * lax.rsqrt(var + CONV_LN_EPS) * lg_ref[...] + lb_ref[...]
    o_ref[0] = (_silu(cn) * _silu(cz_ref[0])).astype(BF16)


def _conv(glu, ext, u, cw, cb, lg, lb):
    b, t, _ = glu.shape
    tt = _row_tile(t, 128)
    assert tt % CONV_HALO == 0 or t == tt
    hb = max(tt // CONV_HALO, 1)
    vec = pl.BlockSpec((1, D_MODEL), lambda bi, i: (0, 0))
    return pl.pallas_call(
        _conv_body,
        grid=(b, t // tt),
        in_specs=[pl.BlockSpec((1, tt, D_MODEL), lambda bi, i: (bi, i, 0)),
                  pl.BlockSpec((1, CONV_HALO, D_MODEL), lambda bi, i: (bi, i * hb, 0)),
                  pl.BlockSpec((1, tt, D_MODEL), lambda bi, i: (bi, i, C_Z)),
                  pl.BlockSpec((CONV_HALO, D_MODEL), lambda bi, i: (0, 0)),
                  vec, vec, vec],
        out_specs=pl.BlockSpec((1, tt, D_MODEL), lambda bi, i: (bi, i, 0)),
        out_shape=jax.ShapeDtypeStruct((b, t, D_MODEL), BF16),
        scratch_shapes=[pltpu.VMEM((CONV_HALO + tt, D_MODEL), F32)],
        compiler_params=_cparams("parallel", "parallel"),
        name="conv",
    )(glu, ext, u, cw, cb, lg, lb)


def _softplus2(x):
    return jnp.maximum(x, 0.0) + jnp.log2(1.0 + jnp.exp2(-jnp.abs(x)))


def _sb_prompt_body(bias_ref, q_ref, k_ref, v_ref, z_ref, tri_ref, o_ref, *, tq, tk, n_split):
    p = pl.program_id(1)
    i = pl.program_id(2)
    tr = tq // n_split
    q2 = q_ref[0]
    lane_q = lax.broadcasted_iota(jnp.int32, (tq, LANES), 1)
    lane_k = lax.broadcasted_iota(jnp.int32, (tk, LANES), 1)
    zero_q = jnp.zeros((tq, LANES), BF16)
    zero_k = jnp.zeros((tk, LANES), BF16)
    q_heads = (jnp.where(lane_q < HEAD_DIM, q2, zero_q), jnp.where(lane_q >= HEAD_DIM, q2, zero_q))
    biases = (bias_ref[2 * p] * LOG2E, bias_ref[2 * p + 1] * LOG2E)
    tri = tri_ref[...]
    row = lax.broadcasted_iota(jnp.int32, (tr, tk), 0)
    colk = lax.broadcasted_iota(jnp.int32, (tr, tk), 1)
    chains = [(h, r) for h in range(2) for r in range(n_split)]
    q_parts = [q_heads[h][r * tr:(r + 1) * tr] for h, r in chains]

    def scores(j):
        kb = k_ref[0, pl.ds(pl.multiple_of(j * tk, tk), tk), :]
        return [lax.dot_general(q, kb, _NT, preferred_element_type=F32) for q in q_parts]

    def weights(s, cs, j, masked):
        z = [x + biases[h] for x, (h, r) in zip(s, chains)]
        sp = [_softplus2(x) for x in z]
        if masked:
            masks = [(colk + j * tk) < (row + (i * tq + r * tr)) for h, r in chains]
            sp = [jnp.where(m, x, 0.0) for m, x in zip(masks, sp)]
        parts = [jnp.concatenate(_split_bf16(x), axis=1) for x in sp]
        later = [jnp.dot(x, tri, preferred_element_type=F32) for x in parts]
        a = [jnp.exp2(zz - x - lt - c) for zz, x, lt, c in zip(z, sp, later, cs)]
        if masked:
            a = [jnp.where(m, x, 0.0) for m, x in zip(masks, a)]
        cs = [c + lt[:, 0:1] + x[:, 0:1] for c, lt, x in zip(cs, later, sp)]
        return [x.astype(BF16) for x in a], cs

    def weighted_values(a, accs, j):
        vb = v_ref[0, pl.ds(pl.multiple_of(j * tk, tk), tk), :]
        v_heads = (jnp.where(lane_k < HEAD_DIM, vb, zero_k), jnp.where(lane_k >= HEAD_DIM, vb, zero_k))
        return [acc + jnp.dot(x, v_heads[h], preferred_element_type=F32) for x, acc, (h, r) in zip(a, accs, chains)]

    cs = [jnp.zeros((tr, 1), F32)] * len(chains)
    accs = [jnp.zeros((tr, LANES), F32)] * len(chains)
    jd = (i * tq) // tk
    a, cs = weights(scores(jd), cs, jd, True)
    s = scores(jnp.maximum(jd - 1, 0))

    def body(it, carry):
        s, a, cs, accs = carry
        j = jd - 1 - it
        s_next = scores(jnp.maximum(j - 1, 0))
        accs = weighted_values(a, accs, j + 1)
        a, cs = weights(s, cs, j, False)
        return s_next, a, cs, accs

    s, a, cs, accs = lax.fori_loop(0, jd, body, (s, a, cs, accs))
    accs = weighted_values(a, accs, 0)
    o = jnp.concatenate([accs[r] + accs[n_split + r] for r in range(n_split)], axis=0)
    o_ref[0] = (o * _silu(z_ref[0])).astype(BF16)


def _sb_prompt(qb, kb, vb, u, bias, tri, tq, tk, n_split):
    b, t, _ = qb.shape
    assert t % tk == 0 and tk % tq == 0 and tq % n_split == 0
    return pl.pallas_call(
        functools.partial(_sb_prompt_body, tq=tq, tk=tk, n_split=n_split),
        grid=(b, N_PAIRS, t // tq),
        in_specs=[pl.BlockSpec(memory_space=pltpu.SMEM),
                  pl.BlockSpec((1, tq, LANES), lambda bi, p, i: (bi, i, p)),
                  pl.BlockSpec((1, t, LANES), lambda bi, p, i: (bi, 0, p)),
                  pl.BlockSpec((1, t, LANES), lambda bi, p, i: (bi, 0, p)),
                  pl.BlockSpec((1, tq, LANES), lambda bi, p, i: (bi, i, C_SBZ * N_PAIRS + p)),
                  pl.BlockSpec((2 * tk, tk), lambda bi, p, i: (0, 0))],
        out_specs=pl.BlockSpec((1, tq, LANES), lambda bi, p, i: (bi, i, p)),
        out_shape=jax.ShapeDtypeStruct((b, t, D_MODEL), BF16),
        compiler_params=_cparams("parallel", "parallel", "arbitrary"),
        name="sb_prompt",
    )(bias, qb, kb, vb, u, tri)


def _sb_sample_body(pt_ref, qt_ref, bias_ref, kn_ref, vn_ref, *rest, n_new, n_vis):
    kp_refs, vp_refs = rest[0:n_vis], rest[n_vis:2 * n_vis]
    z_ref, tri_ref, o_ref, acc_ref, c_ref = rest[2 * n_vis:]
    j = pl.program_id(1)
    qt = qt_ref[0]
    bias = bias_ref[...] * LOG2E
    tri = tri_ref[...]

    def visit(zs, vals, vals_dims, mask):
        zs = [z + bias for z in zs]
        sp = [_softplus2(z) for z in zs]
        if mask is not None:
            sp = [jnp.where(mask, x, 0.0) for x in sp]
        later = [jnp.dot(jnp.concatenate(_split_bf16(x), axis=1), tri, preferred_element_type=F32) for x in sp]
        c = c_ref[...]
        acc = acc_ref[...]
        for z, x, lt, v in zip(zs, sp, later, vals):
            a = jnp.exp2(z - x - lt - c)
            if mask is not None:
                a = jnp.where(mask, a, 0.0)
            acc = acc + lax.dot_general(a.astype(BF16), v, vals_dims, preferred_element_type=F32)
            c = c + lt[:, 0:1] + x[:, 0:1]
        acc_ref[...] = acc
        c_ref[...] = c

    @pl.when(j == 0)
    def _():
        acc_ref[...] = jnp.zeros_like(acc_ref)
        c_ref[...] = jnp.zeros_like(c_ref)
        shape = (qt.shape[0], PAGE)
        mask = lax.broadcasted_iota(jnp.int32, shape, 1) < lax.broadcasted_iota(jnp.int32, shape, 0) % n_new
        visit([lax.dot_general(qt, kn_ref[0], _NT, preferred_element_type=F32)], [vn_ref[0]], _NN, mask)

    visit([jnp.dot(qt, r[0, 0].astype(BF16), preferred_element_type=F32) for r in kp_refs],
          [r[0, 0].astype(BF16) for r in vp_refs], _NT, None)

    @pl.when(j == pl.num_programs(1) - 1)
    def _():
        lane = lax.broadcasted_iota(jnp.int32, (n_new, LANES), 1)
        for m in range(N_PAIRS):
            tile = acc_ref[2 * n_new * m:2 * n_new * (m + 1), LANES * m:LANES * (m + 1)]
            o = jnp.where(lane < HEAD_DIM, tile[0:n_new], tile[n_new:2 * n_new])
            zz = z_ref[0, :, LANES * m:LANES * (m + 1)]
            o_ref[0, :, LANES * m:LANES * (m + 1)] = (o * _silu(zz)).astype(BF16)


def _sb_sample(page_table, qt, bias_rows, k_new, v_new, cache_kt, cache_vt, layer, sbz, tri):
    b, n_pages = page_table.shape
    n_new = sbz.shape[1]
    n_rows = qt.shape[1]
    n_vis = 2 if n_pages % 2 == 0 else 1
    last = n_pages - 1
    seq = lambda rows: pl.BlockSpec((1, rows, D_MODEL), lambda bi, j, pt: (bi, 0, 0))
    pages = [pl.BlockSpec((1, 1, D_MODEL, PAGE), lambda bi, j, pt, v=v: (pt[bi, last - n_vis * j - v], layer, 0, 0))
             for v in range(n_vis)]
    grid_spec = pltpu.PrefetchScalarGridSpec(
        num_scalar_prefetch=1,
        grid=(b, n_pages // n_vis),
        in_specs=[seq(n_rows), pl.BlockSpec((n_rows, 1), lambda bi, j, pt: (0, 0)), seq(PAGE), seq(PAGE)]
        + pages + pages + [seq(n_new), pl.BlockSpec((2 * PAGE, PAGE), lambda bi, j, pt: (0, 0))],
        out_specs=seq(n_new),
        scratch_shapes=[pltpu.VMEM((n_rows, D_MODEL), F32), pltpu.VMEM((n_rows, 1), F32)],
    )
    return pl.pallas_call(
        functools.partial(_sb_sample_body, n_new=n_new, n_vis=n_vis),
        grid_spec=grid_spec,
        out_shape=jax.ShapeDtypeStruct((b, n_new, D_MODEL), BF16),
        compiler_params=_cparams("parallel", "arbitrary"),
        name="sb_sample",
    )(page_table, qt, bias_rows, k_new, v_new, *([cache_kt] * n_vis), *([cache_vt] * n_vis), sbz, tri)


def _rw_prep_body(pr_ref, pk_ref, pv_ref, pl_ref, qr_ref, qk_ref, qv_ref, ql_ref,
                  mu_ref, mul_ref, w0_ref, w2_ref, a0_ref, a2_ref, kkg_ref, ka_ref, rk_ref, seg_ref,
                  r_ref, w_ref, k_ref, v_ref, kk_ref, b_ref, bonus_ref):
    def shift(cur_ref, prev_ref, mu):
        cur = cur_ref[...]
        return cur + (prev_ref[...] - cur) * mu

    r = shift(pr_ref, qr_ref, mu_ref[0:1, :])
    k = shift(pk_ref, qk_ref, mu_ref[1:2, :])
    v = shift(pv_ref, qv_ref, mu_ref[2:3, :])
    lo = shift(pl_ref, ql_ref, mul_ref[...])
    lane = lax.broadcasted_iota(jnp.int32, lo.shape, 1)
    wd = jnp.where(lane < LORA, jnp.tanh(lo), 0.0).astype(BF16)
    ad = jnp.where(lane >= LORA, lo, 0.0).astype(BF16)
    log_decay = -_softplus(-(w0_ref[...] + jnp.dot(wd, w2_ref[...], preferred_element_type=F32))) - 0.5
    a = _sigmoid(a0_ref[...] + jnp.dot(ad, a2_ref[...], preferred_element_type=F32))
    kk = k * kkg_ref[...]
    kk = kk * lax.rsqrt(_seg_sum(kk * kk, seg_ref) + KK_EPS)
    k = k * (1.0 + (a - 1.0) * ka_ref[...])
    r_ref[...] = r
    w_ref[...] = -jnp.exp(log_decay)
    k_ref[...] = k
    v_ref[...] = v
    kk_ref[...] = kk
    b_ref[...] = kk * a
    bonus_ref[...] = _seg_sum(r * k * rk_ref[...], seg_ref) * v


def _rw_prep(u, ul, prev, prev_l, mu3, mul, w0, w2p, a0, a2p, kkg, ka, rk, seg):
    m = u.shape[0]
    tm = _row_tile(m, 256)
    col = lambda c: pl.BlockSpec((tm, D_MODEL), lambda i, c=c: (i, c))
    vec = pl.BlockSpec((1, D_MODEL), lambda i: (0, 0))
    lora = pl.BlockSpec((tm, LANES), lambda i: (i, 0))
    lora_w = pl.BlockSpec((LANES, D_MODEL), lambda i: (0, 0))
    out = pl.BlockSpec((tm, D_MODEL), lambda i: (i, 0))
    return pl.pallas_call(
        _rw_prep_body,
        grid=(m // tm,),
        in_specs=[col(C_R), col(C_RK), col(C_RV), lora,
                  col(0), col(1), col(2), lora,
                  pl.BlockSpec((3, D_MODEL), lambda i: (0, 0)), pl.BlockSpec((1, LANES), lambda i: (0, 0)),
                  vec, lora_w, vec, lora_w, vec, vec, vec,
                  pl.BlockSpec((D_MODEL, D_MODEL), lambda i: (0, 0))],
        out_specs=[out] * 7,
        out_shape=[jax.ShapeDtypeStruct((m, D_MODEL), F32)] * 7,
        compiler_params=_cparams("parallel"),
        name="rw_prep",
    )(u, u, u, ul, prev, prev, prev, prev_l, mu3, mul, w0, w2p, a0, a2p, kkg, ka, rk, seg)


def _rw_scan_body(r_ref, w_ref, k_ref, v_ref, kk_ref, b_ref, s0_ref, y_ref, sT_ref,
                  s_ref, vt_ref, yt_ref, *, n_steps):
    c = pl.program_id(1)
    tc = r_ref.shape[1]

    @pl.when(c == 0)
    def _():
        s_ref[...] = s0_ref[0]

    yt_ref[...] = jnp.zeros_like(yt_ref)
    for p in range(N_PAIRS):
        vt_ref[p] = v_ref[0, :, LANES * p:LANES * (p + 1)].T

    lane_s = lax.broadcasted_iota(jnp.int32, (HEAD_DIM, LANES), 1)
    lane_t = lax.broadcasted_iota(jnp.int32, (HEAD_DIM, tc), 1)
    even = lane_s < HEAD_DIM

    def pair_sum(x):
        se = jnp.sum(jnp.where(even, x, 0.0), axis=1, keepdims=True)
        so = jnp.sum(jnp.where(even, 0.0, x), axis=1, keepdims=True)
        return se, so

    def group(g, carry):
        t0 = pl.multiple_of(g * SUBLANES, SUBLANES)
        for p in range(N_PAIRS):
            sl = slice(LANES * p, LANES * (p + 1))
            rows = [ref[0, pl.ds(t0, SUBLANES), sl] for ref in (kk_ref, w_ref, b_ref, k_ref, r_ref)]
            rows[1] = jnp.exp(rows[1])
            s = s_ref[p]
            for i in range(min(SUBLANES, n_steps)):
                kk_t, w_t, b_t, k_t, r_t = (x[i:i + 1, :] for x in rows)
                hot = lane_t == t0 + i
                sa_e, sa_o = pair_sum(s * kk_t)
                sa = jnp.where(even, sa_e, sa_o)
                v_e = jnp.sum(jnp.where(hot, vt_ref[p, 0:HEAD_DIM, :], 0.0), axis=1, keepdims=True)
                v_o = jnp.sum(jnp.where(hot, vt_ref[p, HEAD_DIM:LANES, :], 0.0), axis=1, keepdims=True)
                vc = jnp.where(even, v_e, v_o)
                s = s * w_t - sa * b_t + vc * k_t
                y_e, y_o = pair_sum(s * r_t)
                yt_ref[p, 0:HEAD_DIM, :] = jnp.where(hot, y_e, yt_ref[p, 0:HEAD_DIM, :])
                yt_ref[p, HEAD_DIM:LANES, :] = jnp.where(hot, y_o, yt_ref[p, HEAD_DIM:LANES, :])
            s_ref[p] = s
        return carry

    lax.fori_loop(0, -(-n_steps // SUBLANES), group, 0)

    for p in range(N_PAIRS):
        y_ref[0, :, LANES * p:LANES * (p + 1)] = yt_ref[p].T

    @pl.when(c == pl.num_programs(1) - 1)
    def _():
        sT_ref[0] = s_ref[...]


def _rw_scan(r, w, k, v, kk, b, s0, n_valid):
    bsz, t, _ = r.shape
    tc = LANES
    assert t % tc == 0 and (n_valid == tc or t == tc)
    assert n_valid % SUBLANES == 0 or n_valid < SUBLANES
    seq = pl.BlockSpec((1, tc, D_MODEL), lambda bi, c: (bi, c, 0))
    st = pl.BlockSpec((1, N_PAIRS, HEAD_DIM, LANES), lambda bi, c: (bi, 0, 0, 0))
    return pl.pallas_call(
        functools.partial(_rw_scan_body, n_steps=n_valid),
        grid=(bsz, t // tc),
        in_specs=[seq] * 6 + [st],
        out_specs=[seq, st],
        out_shape=[jax.ShapeDtypeStruct((bsz, t, D_MODEL), F32),
                   jax.ShapeDtypeStruct((bsz, N_PAIRS, HEAD_DIM, LANES), F32)],
        scratch_shapes=[pltpu.VMEM((N_PAIRS, HEAD_DIM, LANES), F32),
                        pltpu.VMEM((N_PAIRS, LANES, tc), F32),
                        pltpu.VMEM((N_PAIRS, LANES, tc), F32)],
        compiler_params=_cparams("parallel", "arbitrary"),
        name="rw_scan",
    )(r, w, k, v, kk, b, s0)


CHUNK = 64


def _dot3(a, b, dims=_NN):
    ah, al = _split_bf16(a)
    bh, bl = _split_bf16(b)
    d = lambda x, y: lax.dot_general(x, y, dims, preferred_element_type=F32)
    return d(ah, bh) + d(ah, bl) + d(al, bh)


def _unit_lower_inverse(mats, n_head):
    n = mats[0].shape[0]
    r = lax.broadcasted_iota(jnp.int32, (n, n), 0)
    c = lax.broadcasted_iota(jnp.int32, (n, n), 1)
    same = lambda m: (r // m) == (c // m)
    eye = jnp.where(r == c, 1.0, 0.0)
    nb = [jnp.where(same(SUBLANES), a, 0.0) for a in mats]
    nb2 = [_dot3(x, x) for x in nb]
    inv = [eye - x for x in nb]
    inv = [t + _dot3(t, x2) for t, x2 in zip(inv, nb2)]
    nb4 = [_dot3(x2, x2) for x2 in nb2]
    inv = [t + _dot3(t, x4) for t, x4 in zip(inv, nb4)]
    m = SUBLANES
    while m < n_head:
        sel = same(2 * m) & jnp.logical_not(same(m))
        prod = [_dot3(jnp.where(sel, a, 0.0), t) for a, t in zip(mats, inv)]
        inv = [t - _dot3(t, x) for t, x in zip(inv, prod)]
        m *= 2
    return inv


def _rw_chunk_prep_body(r_ref, lw_ref, k_ref, v_ref, kk_ref, b_ref, tri_ref, w3_ref, w4_ref, g_ref, h_ref):
    cs = r_ref.shape[1]
    lw = lw_ref[0]
    hi, lo = _split_bf16(lw)
    tri = tri_ref[...]
    cum = jnp.dot(tri, hi, preferred_element_type=F32) + jnp.dot(tri, lo, preferred_element_type=F32)
    cum_end = cum[cs - 1:cs, :]
    grow = jnp.exp(-cum)
    to_end = jnp.exp(cum_end - cum)
    k = k_ref[0]
    b = b_ref[0]
    kt_all = k * grow
    bt_all = b * grow
    kd_all = k * to_end
    bd_all = b * to_end
    kkh_all = kk_ref[0] * jnp.exp(cum - lw)
    rh_all = r_ref[0] * jnp.exp(cum)
    dend_all = jnp.exp(cum_end)
    v_all = v_ref[0]

    n = 2 * cs
    lane = lax.broadcasted_iota(jnp.int32, (cs, LANES), 1)
    even = lane < HEAD_DIM
    row = lax.broadcasted_iota(jnp.int32, (n, n), 0)
    col = lax.broadcasted_iota(jnp.int32, (n, n), 1)
    same_head = (row // cs) == (col // cs)
    strict = same_head & ((col % cs) < (row % cs))
    incl = same_head & ((col % cs) <= (row % cs))
    diag = lax.broadcasted_iota(jnp.int32, (LANES, LANES), 0) == lax.broadcasted_iota(jnp.int32, (LANES, LANES), 1)
    pairs = range(N_PAIRS)
    sl = [slice(LANES * p, LANES * (p + 1)) for p in pairs]

    def embed(x):
        return [jnp.concatenate([jnp.where(even, x[:, s], 0.0), jnp.where(even, 0.0, x[:, s])], axis=0) for s in sl]

    kt, bt, kd, bd, kkh, rh, v = (embed(x) for x in (kt_all, bt_all, kd_all, bd_all, kkh_all, rh_all, v_all))
    a = [_dot3(jnp.concatenate([kkh[p], rh[p]], axis=0), jnp.concatenate([kt[p], bt[p]], axis=0), _NT)
         for p in pairs]
    a_kk = [jnp.where(strict, x[0:n, 0:n], 0.0) for x in a]
    a_kb = [jnp.where(strict, x[0:n, n:2 * n], 0.0) for x in a]
    a_rk = [jnp.where(incl, x[n:2 * n, 0:n], 0.0) for x in a]
    a_rb = [jnp.where(incl, x[n:2 * n, n:2 * n], 0.0) for x in a]
    pv = [_dot3(jnp.concatenate([a_kk[p], a_rk[p]], axis=0), v[p]) for p in pairs]
    hkv = [_dot3(kd[p], v[p], _TN) for p in pairs]
    inv = _unit_lower_inverse(a_kb, cs)
    w12 = [_dot3(inv[p], jnp.concatenate([kkh[p], pv[p][0:n]], axis=1)) for p in pairs]
    q12 = [_dot3(a_rb[p], w12[p]) for p in pairs]
    e12 = [_dot3(bd[p], w12[p], _TN) for p in pairs]
    for p in pairs:
        w3 = rh[p] - q12[p][:, 0:LANES]
        w4 = pv[p][n:2 * n] - q12[p][:, LANES:2 * LANES]
        w3_ref[0, :, sl[p]] = w3[0:cs] + w3[cs:n]
        w4_ref[0, :, sl[p]] = w4[0:cs] + w4[cs:n]
        g_ref[0, 0, p] = jnp.where(diag, dend_all[:, sl[p]], 0.0) - e12[p][:, 0:LANES]
        h_ref[0, 0, p] = hkv[p] - e12[p][:, LANES:2 * LANES]


def _rw_chunk_prep(r, lw, k, v, kk, b):
    bsz, t, _ = r.shape
    nc = t // CHUNK
    idx = jnp.arange(CHUNK)
    tri = (idx[None, :] <= idx[:, None]).astype(BF16)
    seq = pl.BlockSpec((1, CHUNK, D_MODEL), lambda bi, c: (bi, c, 0))
    mat = pl.BlockSpec((1, 1, N_PAIRS, LANES, LANES), lambda bi, c: (bi, c, 0, 0, 0))
    mat_shape = jax.ShapeDtypeStruct((bsz, nc, N_PAIRS, LANES, LANES), F32)
    return pl.pallas_call(
        _rw_chunk_prep_body,
        grid=(bsz, nc),
        in_specs=[seq] * 6 + [pl.BlockSpec((CHUNK, CHUNK), lambda bi, c: (0, 0))],
        out_specs=[seq, seq, mat, mat],
        out_shape=[jax.ShapeDtypeStruct((bsz, t, D_MODEL), F32)] * 2 + [mat_shape] * 2,
        compiler_params=_cparams("parallel", "parallel"),
        name="rw_chunk_prep",
    )(r, lw, k, v, kk, b, tri)


def _rw_chunk_scan_body(w3_ref, w4_ref, g_ref, h_ref, z0_ref, y_ref, zT_ref, z_ref):
    c = pl.program_id(1)
    cs = w3_ref.shape[1]

    @pl.when(c == 0)
    def _():
        z_ref[...] = z0_ref[0]

    for p in range(N_PAIRS):
        sl = slice(LANES * p, LANES * (p + 1))
        out = _dot3(jnp.concatenate([w3_ref[0, :, sl], g_ref[0, 0, p]], axis=0), z_ref[p])
        y_ref[0, :, sl] = out[0:cs] + w4_ref[0, :, sl]
        z_ref[p] = out[cs:cs + LANES] + h_ref[0, 0, p]

    @pl.when(c == pl.num_programs(1) - 1)
    def _():
        zT_ref[0] = z_ref[...]


def _rw_chunk_scan(w3, w4, g, h, z0):
    bsz, t, _ = w3.shape
    seq = pl.BlockSpec((1, CHUNK, D_MODEL), lambda bi, c: (bi, c, 0))
    mat = pl.BlockSpec((1, 1, N_PAIRS, LANES, LANES), lambda bi, c: (bi, c, 0, 0, 0))
    st = pl.BlockSpec((1, N_PAIRS, LANES, LANES), lambda bi, c: (bi, 0, 0, 0))
    return pl.pallas_call(
        _rw_chunk_scan_body,
        grid=(bsz, t // CHUNK),
        in_specs=[seq, seq, mat, mat, st],
        out_specs=[seq, st],
        out_shape=[jax.ShapeDtypeStruct((bsz, t, D_MODEL), F32),
                   jax.ShapeDtypeStruct((bsz, N_PAIRS, LANES, LANES), F32)],
        scratch_shapes=[pltpu.VMEM((N_PAIRS, LANES, LANES), F32)],
        compiler_params=_cparams("parallel", "arbitrary"),
        name="rw_chunk_scan",
    )(w3, w4, g, h, z0)


def _state_to_blockdiag(s):
    b = s.shape[0]
    st = jnp.swapaxes(s, 2, 3).reshape(b, N_PAIRS, 2, HEAD_DIM, HEAD_DIM)
    z = jnp.zeros((b, N_PAIRS, 2, HEAD_DIM, 2, HEAD_DIM), F32)
    z = z.at[:, :, 0, :, 0, :].set(st[:, :, 0]).at[:, :, 1, :, 1, :].set(st[:, :, 1])
    return z.reshape(b, N_PAIRS, LANES, LANES)


def _blockdiag_to_state(z):
    b = z.shape[0]
    z6 = z.reshape(b, N_PAIRS, 2, HEAD_DIM, 2, HEAD_DIM)
    st = jnp.stack([z6[:, :, 0, :, 0, :], z6[:, :, 1, :, 1, :]], axis=2)
    return jnp.swapaxes(st, 3, 4).reshape(b, N_HEADS, HEAD_DIM, HEAD_DIM)


def _rw_post_body(y_ref, bonus_ref, z_ref, g_ref, b_ref, seg_ref, o_ref):
    y = y_ref[...]
    mu = _seg_sum(y, seg_ref) * (1.0 / HEAD_DIM)
    d = y - mu
    var = _seg_sum(d * d, seg_ref) * (1.0 / HEAD_DIM)
    yn = d * lax.rsqrt(var + LNX_EPS) * g_ref[...] + b_ref[...]
    o_ref[...] = ((yn + bonus_ref[...]) * _silu(z_ref[...])).astype(BF16)


def _rw_post(y, bonus, u, g, b, seg):
    m = y.shape[0]
    tm = _row_tile(m, 512)
    row = pl.BlockSpec((tm, D_MODEL), lambda i: (i, 0))
    vec = pl.BlockSpec((1, D_MODEL), lambda i: (0, 0))
    return pl.pallas_call(
        _rw_post_body,
        grid=(m // tm,),
        in_specs=[row, row, pl.BlockSpec((tm, D_MODEL), lambda i: (i, C_RZ)), vec, vec,
                  pl.BlockSpec((D_MODEL, D_MODEL), lambda i: (0, 0))],
        out_specs=row,
        out_shape=jax.ShapeDtypeStruct((m, D_MODEL), BF16),
        compiler_params=_cparams("parallel"),
        name="rw_post",
    )(y, bonus, u, g, b, seg)


def _merge_body(x_ref, oa_ref, ob_ref, oc_ref, g0_ref, g1_ref, g2_ref,
                wa_ref, wb_ref, wc_ref, wo_ref, o_ref):
    merged = (_sigmoid(g0_ref[...]) * jnp.dot(oa_ref[...], wa_ref[...], preferred_element_type=F32)
              + _sigmoid(g1_ref[...]) * jnp.dot(ob_ref[...], wb_ref[...], preferred_element_type=F32)
              + _sigmoid(g2_ref[...]) * jnp.dot(oc_ref[...], wc_ref[...], preferred_element_type=F32))
    o_ref[...] = x_ref[...] + jnp.dot(merged.astype(BF16), wo_ref[...], preferred_element_type=F32)


def _merge(x, oa, ob, oc, u, wa, wb, wc, wo):
    m = x.shape[0]
    tm = _row_tile(m, 512)
    row = pl.BlockSpec((tm, D_MODEL), lambda i: (i, 0))
    col = lambda c: pl.BlockSpec((tm, D_MODEL), lambda i, c=c: (i, c))
    wsp = pl.BlockSpec((D_MODEL, D_MODEL), lambda i: (0, 0))
    return pl.pallas_call(
        _merge_body,
        grid=(m // tm,),
        in_specs=[row, row, row, row, col(C_G0), col(C_G1), col(C_G2), wsp, wsp, wsp, wsp],
        out_specs=row,
        out_shape=jax.ShapeDtypeStruct((m, D_MODEL), F32),
        compiler_params=_cparams("parallel"),
        name="merge",
    )(x, oa, ob, oc, u, u, u, wa, wb, wc, wo)


def _later_matrix(n, lhs):
    i = jnp.arange(n)
    m = (i[:, None] > i[None, :]) if not lhs else (i[None, :] > i[:, None])
    return m.astype(BF16)


def _pack_pairs(s):
    b = s.shape[0]
    return s.reshape(b, N_PAIRS, 2, HEAD_DIM, HEAD_DIM).transpose(0, 1, 3, 2, 4).reshape(b, N_PAIRS, HEAD_DIM, LANES)


def _unpack_pairs(s):
    b = s.shape[0]
    return s.reshape(b, N_PAIRS, HEAD_DIM, 2, HEAD_DIM).transpose(0, 1, 3, 2, 4).reshape(b, N_HEADS, HEAD_DIM, HEAD_DIM)


def _layer(x, conv_prev, shift_prev, wkv_prev, lw, attend):
    b, t, _ = x.shape
    m = b * t
    x2 = x.reshape(m, D_MODEL)
    u = _in_proj(x2, lw['norm_g'], lw['w_main'], 1024)
    ul = _in_proj(x2, lw['norm_g'], lw['w_lora'], LANES)

    glu, qb, kf, kb, vb = _pre(u, lw['q_g'], lw['k_g'], lw['seg'])
    u3 = u.reshape(b, t, N_MAIN)

    glu3 = glu.reshape(b, t, D_MODEL)
    hist = jnp.concatenate([jnp.zeros((b, CONV_HALO - (CONV_W - 1), D_MODEL), F32), conv_prev], axis=1)
    ext = jnp.concatenate([hist, glu3], axis=1)
    oa = _conv(glu3, ext, u3, lw['conv_w'], lw['conv_b'], lw['conv_ln_g'], lw['conv_ln_b'])
    conv_new = ext[:, -(CONV_W - 1):]

    ob = attend(qb.reshape(b, t, D_MODEL), kb.reshape(b, t, D_MODEL), vb.reshape(b, t, D_MODEL), u3)
    v_out = u3[:, :, C_V * D_MODEL:(C_V + 1) * D_MODEL]

    p_main = u3[:, :, C_R * D_MODEL:(C_RV + 1) * D_MODEL]
    ul3 = ul.reshape(b, t, LANES)
    prev = jnp.concatenate([shift_prev[:, None, :3 * D_MODEL], p_main[:, :-1]], axis=1).reshape(m, 3 * D_MODEL)
    prev_l = jnp.concatenate([shift_prev[:, None, 3 * D_MODEL:], ul3[:, :-1]], axis=1).reshape(m, LANES)
    r, w, k, v, kk, bb, bonus = _rw_prep(u, ul, prev, prev_l, lw['mu3'], lw['mul'], lw['w0'], lw['w2p'],
                                         lw['a0'], lw['a2p'], lw['k_k'], lw['k_a'], lw['r_k'], lw['seg'])
    seqs = [a.reshape(b, t, D_MODEL) for a in (r, w, k, v, kk, bb)]
    if t % CHUNK == 0:
        w3, w4, g, h = _rw_chunk_prep(*seqs)
        y, z_new = _rw_chunk_scan(w3, w4, g, h, _state_to_blockdiag(wkv_prev))
        s_new = _blockdiag_to_state(z_new)
    else:
        assert t < SUBLANES
        seqs = [jnp.pad(a, ((0, 0), (0, LANES - t), (0, 0))) for a in seqs]
        y, s_new = _rw_scan(*seqs, _pack_pairs(wkv_prev), t)
        y, s_new = y[:, :t], _unpack_pairs(s_new)
    y = y.reshape(m, D_MODEL)
    oc = _rw_post(y, bonus, u, lw['lnx_g'], lw['lnx_b'], lw['seg'])
    shift_new = jnp.concatenate([p_main[:, -1], ul3[:, -1]], axis=-1)

    x_new = _merge(x2, oa.reshape(m, D_MODEL), ob.reshape(m, D_MODEL), oc, u,
                   lw['w_pa'], lw['w_pb'], lw['w_pc'], lw['w_out'])
    return (x_new.reshape(b, t, D_MODEL), kf.reshape(b, t, N_HEADS, HEAD_DIM),
            v_out.reshape(b, t, N_HEADS, HEAD_DIM), s_new, shift_new, conv_new)


def kernel(x_prompt, x_sample, cache_k, cache_v, state_wkv, state_shift, state_conv, page_table, norm_g, w_in,
           conv_w, conv_b, conv_ln_g, conv_ln_b, w_pa, q_norm_g, k_norm_g, sb_bias, w_pb, mu_shift, w0, w2, a0,
           a2, k_k, k_a, r_k, lnx_g, lnx_b, w_pc, w_out):
    depth = w_in.shape[0]
    nb_p, t_p, _ = x_prompt.shape
    nb_s, t_s, _ = x_sample.shape
    n_pool = cache_k.shape[0]
    lora0 = 10 * D_MODEL
    seg = jnp.kron(jnp.eye(N_HEADS, dtype=F32), jnp.ones((HEAD_DIM, HEAD_DIM), F32)).astype(BF16)
    vec = lambda a: a.reshape(1, -1)
    cache_kt = jnp.transpose(cache_k, (0, 1, 3, 4, 2)).reshape(n_pool, depth, D_MODEL, PAGE)
    cache_vt = jnp.transpose(cache_v, (0, 1, 3, 4, 2)).reshape(n_pool, depth, D_MODEL, PAGE)
    assert t_s <= PAGE
    tri_prompt = jnp.tile(_later_matrix(SB_TK, lhs=False), (2, 1))
    tri_page = jnp.tile(_later_matrix(PAGE, lhs=False), (2, 1))
    qrow_head = jnp.arange(N_HEADS * t_s) // t_s
    lane_head = jnp.arange(D_MODEL) // HEAD_DIM

    x_p, x_s = x_prompt, x_sample
    outs_p = [[] for _ in range(5)]
    outs_s = [[] for _ in range(5)]
    for l in range(depth):
        wl = w_in[l]
        zpad = jnp.zeros((LORA, D_MODEL), F32)
        lw = {
            'norm_g': vec(norm_g[l]),
            'w_main': jnp.concatenate([wl[:, :lora0], wl[:, lora0 + 2 * LORA:]], axis=1).astype(BF16),
            'w_lora': wl[:, lora0:lora0 + 2 * LORA].astype(BF16),
            'q_g': vec(jnp.tile(q_norm_g[l], N_HEADS)), 'k_g': vec(jnp.tile(k_norm_g[l], N_HEADS)),
            'seg': seg,
            'conv_w': jnp.concatenate([conv_w[l], jnp.zeros((CONV_HALO - CONV_W, D_MODEL), F32)], axis=0),
            'conv_b': vec(conv_b[l]), 'conv_ln_g': vec(conv_ln_g[l]), 'conv_ln_b': vec(conv_ln_b[l]),
            'mu3': mu_shift[l, :3 * D_MODEL].reshape(3, D_MODEL), 'mul': vec(mu_shift[l, 3 * D_MODEL:]),
            'w0': vec(w0[l]), 'w2p': jnp.concatenate([w2[l], zpad], axis=0).astype(BF16),
            'a0': vec(a0[l]), 'a2p': jnp.concatenate([zpad, a2[l]], axis=0).astype(BF16),
            'k_k': vec(k_k[l]), 'k_a': vec(k_a[l]), 'r_k': vec(r_k[l]),
            'lnx_g': vec(lnx_g[l]), 'lnx_b': vec(lnx_b[l]),
            'w_pa': w_pa[l].astype(BF16), 'w_pb': w_pb[l].astype(BF16), 'w_pc': w_pc[l].astype(BF16),
            'w_out': w_out[l].astype(BF16),
        }
        bias_l = sb_bias[l]

        def attend_prompt(qb, kb, vb, u3):
            return _sb_prompt(qb, kb, vb, u3, bias_l, tri_prompt, SB_TQ, SB_TK, SB_SPLIT)

        def attend_sample(qb, kb, vb, u3):
            qt = jnp.tile(qb, (1, N_HEADS, 1))
            qt = jnp.where(qrow_head[:, None] == lane_head[None, :], qt, jnp.zeros_like(qt))
            pad = ((0, 0), (0, PAGE - t_s), (0, 0))
            sbz = u3[:, :, C_SBZ * D_MODEL:(C_SBZ + 1) * D_MODEL]
            return _sb_sample(page_table, qt, jnp.repeat(bias_l, t_s)[:, None], jnp.pad(kb, pad), jnp.pad(vb, pad),
                              cache_kt, cache_vt, l, sbz, tri_page)

        res_p = _layer(x_p, jnp.zeros((nb_p, CONV_W - 1, D_MODEL), F32), jnp.zeros((nb_p, state_shift.shape[2]), F32),
                       jnp.zeros((nb_p,) + state_wkv.shape[2:], F32), lw, attend_prompt)
        res_s = _layer(x_s, state_conv[:, l], state_shift[:, l], state_wkv[:, l], lw, attend_sample)
        x_p, x_s = res_p[0], res_s[0]
        for i in range(5):
            outs_p[i].append(res_p[i + 1])
            outs_s[i].append(res_s[i + 1])
    stack = lambda xs: jnp.stack(xs, axis=1)
    return (x_p, x_s) + tuple(stack(o) for o in outs_p) + tuple(stack(o) for o in outs_s)
```

```python
import functools

import jax
import jax.numpy as jnp
from jax import lax
from jax.experimental import pallas as pl
from jax.experimental.pallas import tpu as pltpu

F32 = jnp.float32
BF16 = jnp.bfloat16

D_MODEL = 1024
N_HEADS = 16
HEAD_DIM = 64
N_PAIRS = N_HEADS // 2
LANES = 128
SUBLANES = 8
CONV_W = 31
CONV_HALO = 32
LORA = 64
PAGE = 128
NORM_EPS = 1e-6
CONV_LN_EPS = 1e-5
LNX_EPS = 64e-5
KK_EPS = 1e-12
SB_SCALE = HEAD_DIM ** -0.5
LOG2E = 1.4426950408889634
_NT = (((1,), (1,)), ((), ()))
_TN = (((0,), (0,)), ((), ()))
_NN = (((1,), (0,)), ((), ()))

C_VAL, C_GATE, C_Z, C_Q, C_K, C_V, C_SBZ, C_R, C_RK, C_RV, C_RZ, C_G0, C_G1, C_G2 = range(14)
N_MAIN = 14 * D_MODEL

VMEM_LIMIT = 48 * 1024 * 1024
SB_TQ, SB_TK, SB_SPLIT = 256, 256, 1
SLAB = 32


def _cparams(*sem):
    return pltpu.CompilerParams(dimension_semantics=sem, vmem_limit_bytes=VMEM_LIMIT)


def _sigmoid(x):
    return 1.0 / (1.0 + jnp.exp(-x))


def _silu(x):
    return x * _sigmoid(x)


def _softplus(x):
    return jnp.maximum(x, 0.0) + jnp.log(1.0 + jnp.exp(-jnp.abs(x)))


def _split_bf16(x):
    hi = x.astype(BF16)
    lo = (x - hi.astype(F32)).astype(BF16)
    return hi, lo


def _seg_sum(x, seg_ref):
    hi, lo = _split_bf16(x)
    seg = seg_ref[...]
    return (jnp.dot(hi, seg, preferred_element_type=F32)
            + jnp.dot(lo, seg, preferred_element_type=F32))


def _row_tile(m, pref):
    t = min(m, pref)
    assert m % t == 0
    return t


def _in_proj_body(x_ref, g_ref, w_ref, o_ref, h_ref):
    @pl.when(pl.program_id(1) == 0)
    def _():
        x = x_ref[...]
        ms = jnp.mean(x * x, axis=-1, keepdims=True)
        h_ref[...] = (x * lax.rsqrt(ms + NORM_EPS) * g_ref[...]).astype(BF16)

    o_ref[...] = jnp.dot(h_ref[...], w_ref[...], preferred_element_type=F32)


def _in_proj(x, g, w, tn):
    m, n = x.shape[0], w.shape[1]
    tm = _row_tile(m, 1024)
    return pl.pallas_call(
        _in_proj_body,
        grid=(m // tm, n // tn),
        in_specs=[pl.BlockSpec((tm, D_MODEL), lambda i, j: (i, 0)),
                  pl.BlockSpec((1, D_MODEL), lambda i, j: (0, 0)),
                  pl.BlockSpec((D_MODEL, tn), lambda i, j: (0, j))],
        out_specs=pl.BlockSpec((tm, tn), lambda i, j: (i, j)),
        out_shape=jax.ShapeDtypeStruct((m, n), F32),
        scratch_shapes=[pltpu.VMEM((tm, D_MODEL), BF16)],
        compiler_params=_cparams("parallel", "arbitrary"),
        name="in_proj",
    )(x, g, w)


def _pre_body(cv_ref, cg_ref, q_ref, k_ref, v_ref, qg_ref, kg_ref, seg_ref,
              glu_ref, qb_ref, kf_ref, kb_ref, vb_ref):
    glu_ref[...] = cv_ref[...] * _sigmoid(cg_ref[...])
    q = q_ref[...]
    k = k_ref[...]
    q_ms = _seg_sum(q * q, seg_ref) * (1.0 / HEAD_DIM)
    k_ms = _seg_sum(k * k, seg_ref) * (1.0 / HEAD_DIM)
    qn = q * lax.rsqrt(q_ms + NORM_EPS) * qg_ref[...]
    kn = k * lax.rsqrt(k_ms + NORM_EPS) * kg_ref[...]
    qb_ref[...] = (qn * (SB_SCALE * LOG2E)).astype(BF16)
    kf_ref[...] = kn
    kb_ref[...] = kn.astype(BF16)
    vb_ref[...] = v_ref[...].astype(BF16)


def _pre(u, qg, kg, seg):
    m = u.shape[0]
    tm = _row_tile(m, 512)
    col = lambda c: pl.BlockSpec((tm, D_MODEL), lambda i, c=c: (i, c))
    vec = pl.BlockSpec((1, D_MODEL), lambda i: (0, 0))
    out = pl.BlockSpec((tm, D_MODEL), lambda i: (i, 0))
    sds = lambda dt: jax.ShapeDtypeStruct((m, D_MODEL), dt)
    return pl.pallas_call(
        _pre_body,
        grid=(m // tm,),
        in_specs=[col(C_VAL), col(C_GATE), col(C_Q), col(C_K), col(C_V), vec, vec,
                  pl.BlockSpec((D_MODEL, D_MODEL), lambda i: (0, 0))],
        out_specs=[out] * 5,
        out_shape=[sds(F32), sds(BF16), sds(F32), sds(BF16), sds(BF16)],
        compiler_params=_cparams("parallel"),
        name="pre",
    )(u, u, u, u, u, qg, kg, seg)


def _conv_body(cur_ref, halo_ref, cz_ref, cw_ref, cb_ref, lg_ref, lb_ref, o_ref, ext_ref):
    tt = cur_ref.shape[1]
    ext_ref[0:CONV_HALO, :] = halo_ref[0]
    ext_ref[CONV_HALO:CONV_HALO + tt, :] = cur_ref[0]
    first = CONV_HALO - (CONV_W - 1)
    acc = jnp.zeros((tt, D_MODEL), F32)
    for w in range(CONV_W):
        acc = acc + cw_ref[w:w + 1, :] * ext_ref[first + w:first + w + tt, :]
    c = acc + cb_ref[...]
    mu = jnp.mean(c, axis=-1, keepdims=True)
    d = c - mu
    var = jnp.mean(d * d, axis=-1, keepdims=True)
    cn = d * lax.rsqrt(var + CONV_LN_EPS) * lg_ref[...] + lb_ref[...]
    o_ref[0] = (_silu(cn) * _silu(cz_ref[0])).astype(BF16)


def _conv(glu, ext, u, cw, cb, lg, lb):
    b, t, _ = glu.shape
    tt = _row_tile(t, 128)
    assert tt % CONV_HALO == 0 or t == tt
    hb = max(tt // CONV_HALO, 1)
    vec = pl.BlockSpec((1, D_MODEL), lambda bi, i: (0, 0))
    return pl.pallas_call(
        _conv_body,
        grid=(b, t // tt),
        in_specs=[pl.BlockSpec((1, tt, D_MODEL), lambda bi, i: (bi, i, 0)),
                  pl.BlockSpec((1, CONV_HALO, D_MODEL), lambda bi, i: (bi, i * hb, 0)),
                  pl.BlockSpec((1, tt, D_MODEL), lambda bi, i: (bi, i, C_Z)),
                  pl.BlockSpec((CONV_HALO, D_MODEL), lambda bi, i: (0, 0)),
                  vec, vec, vec],
        out_specs=pl.BlockSpec((1, tt, D_MODEL), lambda bi, i: (bi, i, 0)),
        out_shape=jax.ShapeDtypeStruct((b, t, D_MODEL), BF16),
        scratch_shapes=[pltpu.VMEM((CONV_HALO + tt, D_MODEL), F32)],
        compiler_params=_cparams("parallel", "parallel"),
        name="conv",
    )(glu, ext, u, cw, cb, lg, lb)


def _softplus2(x):
    return jnp.maximum(x, 0.0) + jnp.log2(1.0 + jnp.exp2(-jnp.abs(x)))


def _sb_prompt_body(bias_ref, q_ref, k_ref, v_ref, z_ref, tri_ref, o_ref, sp_scr, base_scr, a_scr, first_scr,
                    *, tq, tk, n_split):
    p = pl.program_id(1)
    i = pl.program_id(2)
    tr = tq // n_split
    q2 = q_ref[0]
    lane_q = lax.broadcasted_iota(jnp.int32, (tq, LANES), 1)
    lane_k = lax.broadcasted_iota(jnp.int32, (tk, LANES), 1)
    zero_q = jnp.zeros((tq, LANES), BF16)
    zero_k = jnp.zeros((tk, LANES), BF16)
    q_heads = (jnp.where(lane_q < HEAD_DIM, q2, zero_q), jnp.where(lane_q >= HEAD_DIM, q2, zero_q))
    biases = (bias_ref[2 * p] * LOG2E, bias_ref[2 * p + 1] * LOG2E)
    tri = tri_ref[...]
    row = lax.broadcasted_iota(jnp.int32, (SLAB, tk), 0)
    colk = lax.broadcasted_iota(jnp.int32, (SLAB, tk), 1)
    chains = [(h, r) for h in range(2) for r in range(n_split)]
    q_parts = [q_heads[h][r * tr:(r + 1) * tr] for h, r in chains]

    def scores(j):
        kb = k_ref[0, pl.ds(pl.multiple_of(j * tk, tk), tk), :]
        return [lax.dot_general(q, kb, _NT, preferred_element_type=F32) for q in q_parts]

    def weights(s, cs, j, masked):
        c_out = []
        for n, (x, c, (h, r)) in enumerate(zip(s, cs, chains)):
            for r0 in range(0, tr, SLAB):
                rows = slice(r0, r0 + SLAB)
                z = x[rows] + biases[h]
                sp = _softplus2(z)
                if masked:
                    sp = jnp.where((colk + j * tk) < (row + (i * tq + r * tr + r0)), sp, 0.0)
                sp_scr[n, rows] = sp.astype(BF16)
                base_scr[n, rows] = z - sp
                first_scr[n, rows] = sp[:, 0:LANES]
            later = jnp.dot(sp_scr[n], tri, preferred_element_type=F32)
            total = later + c
            for r0 in range(0, tr, SLAB):
                rows = slice(r0, r0 + SLAB)
                a = jnp.exp2(base_scr[n, rows] - total[rows])
                if masked:
                    a = jnp.where((colk + j * tk) < (row + (i * tq + r * tr + r0)), a, 0.0)
                a_scr[n, rows] = a.astype(BF16)
            c_out.append(total[:, 0:1] + first_scr[n][:, 0:1])
        return c_out

    def weighted_values(accs, j):
        vb = v_ref[0, pl.ds(pl.multiple_of(j * tk, tk), tk), :]
        v_heads = (jnp.where(lane_k < HEAD_DIM, vb, zero_k), jnp.where(lane_k >= HEAD_DIM, vb, zero_k))
        return [acc + jnp.dot(a_scr[n], v_heads[h], preferred_element_type=F32)
                for n, (acc, (h, r)) in enumerate(zip(accs, chains))]

    cs = [jnp.zeros((tr, 1), F32)] * len(chains)
    accs = [jnp.zeros((tr, LANES), F32)] * len(chains)
    jd = (i * tq) // tk
    cs = weights(scores(jd), cs, jd, True)
    s = scores(jnp.maximum(jd - 1, 0))

    def body(it, carry):
        s, cs, accs = carry
        j = jd - 1 - it
        s_next = scores(jnp.maximum(j - 1, 0))
        accs = weighted_values(accs, j + 1)
        cs = weights(s, cs, j, False)
        return s_next, cs, accs

    s, cs, accs = lax.fori_loop(0, jd, body, (s, cs, accs))
    accs = weighted_values(accs, 0)
    o = jnp.concatenate([accs[r] + accs[n_split + r] for r in range(n_split)], axis=0)
    o_ref[0] = (o * _silu(z_ref[0])).astype(BF16)


def _sb_prompt(qb, kb, vb, u, bias, tri, tq, tk, n_split):
    b, t, _ = qb.shape
    assert t % tk == 0 and tk % tq == 0 and tq % n_split == 0
    return pl.pallas_call(
        functools.partial(_sb_prompt_body, tq=tq, tk=tk, n_split=n_split),
        grid=(b, N_PAIRS, t // tq),
        in_specs=[pl.BlockSpec(memory_space=pltpu.SMEM),
                  pl.BlockSpec((1, tq, LANES), lambda bi, p, i: (bi, i, p)),
                  pl.BlockSpec((1, t, LANES), lambda bi, p, i: (bi, 0, p)),
                  pl.BlockSpec((1, t, LANES), lambda bi, p, i: (bi, 0, p)),
                  pl.BlockSpec((1, tq, LANES), lambda bi, p, i: (bi, i, C_SBZ * N_PAIRS + p)),
                  pl.BlockSpec((tk, tk), lambda bi, p, i: (0, 0))],
        out_specs=pl.BlockSpec((1, tq, LANES), lambda bi, p, i: (bi, i, p)),
        out_shape=jax.ShapeDtypeStruct((b, t, D_MODEL), BF16),
        scratch_shapes=[pltpu.VMEM((2 * n_split, tq // n_split, tk), BF16),
                        pltpu.VMEM((2 * n_split, tq // n_split, tk), F32),
                        pltpu.VMEM((2 * n_split, tq // n_split, tk), BF16),
                        pltpu.VMEM((2 * n_split, tq // n_split, LANES), F32)],
        compiler_params=_cparams("parallel", "parallel", "arbitrary"),
        name="sb_prompt",
    )(bias, qb, kb, vb, u, tri)


def _sb_sample_body(pt_ref, qt_ref, bias_ref, kn_ref, vn_ref, *rest, n_new, n_vis):
    kp_refs, vp_refs = rest[0:n_vis], rest[n_vis:2 * n_vis]
    z_ref, tri_ref, o_ref, acc_ref, c_ref = rest[2 * n_vis:]
    j = pl.program_id(1)
    qt = qt_ref[0]
    bias = bias_ref[...] * LOG2E
    tri = tri_ref[...]

    def visit(zs, vals, vals_dims, mask):
        zs = [z + bias for z in zs]
        sp = [_softplus2(z) for z in zs]
        if mask is not None:
            sp = [jnp.where(mask, x, 0.0) for x in sp]
        later = [jnp.dot(jnp.concatenate(_split_bf16(x), axis=1), tri, preferred_element_type=F32) for x in sp]
        c = c_ref[...]
        acc = acc_ref[...]
        for z, x, lt, v in zip(zs, sp, later, vals):
            a = jnp.exp2(z - x - lt - c)
            if mask is not None:
                a = jnp.where(mask, a, 0.0)
            acc = acc + lax.dot_general(a.astype(BF16), v, vals_dims, preferred_element_type=F32)
            c = c + lt[:, 0:1] + x[:, 0:1]
        acc_ref[...] = acc
        c_ref[...] = c

    @pl.when(j == 0)
    def _():
        acc_ref[...] = jnp.zeros_like(acc_ref)
        c_ref[...] = jnp.zeros_like(c_ref)
        shape = (qt.shape[0], PAGE)
        mask = lax.broadcasted_iota(jnp.int32, shape, 1) < lax.broadcasted_iota(jnp.int32, shape, 0) % n_new
        visit([lax.dot_general(qt, kn_ref[0], _NT, preferred_element_type=F32)], [vn_ref[0]], _NN, mask)

    visit([jnp.dot(qt, r[0, 0].astype(BF16), preferred_element_type=F32) for r in kp_refs],
          [r[0, 0].astype(BF16) for r in vp_refs], _NT, None)

    @pl.when(j == pl.num_programs(1) - 1)
    def _():
        lane = lax.broadcasted_iota(jnp.int32, (n_new, LANES), 1)
        for m in range(N_PAIRS):
            tile = acc_ref[2 * n_new * m:2 * n_new * (m + 1), LANES * m:LANES * (m + 1)]
            o = jnp.where(lane < HEAD_DIM, tile[0:n_new], tile[n_new:2 * n_new])
            zz = z_ref[0, :, LANES * m:LANES * (m + 1)]
            o_ref[0, :, LANES * m:LANES * (m + 1)] = (o * _silu(zz)).astype(BF16)


def _sb_sample(page_table, qt, bias_rows, k_new, v_new, cache_kt, cache_vt, layer, sbz, tri):
    b, n_pages = page_table.shape
    n_new = sbz.shape[1]
    n_rows = qt.shape[1]
    n_vis = max(v for v in (4, 2, 1) if n_pages % v == 0)
    last = n_pages - 1
    seq = lambda rows: pl.BlockSpec((1, rows, D_MODEL), lambda bi, j, pt: (bi, 0, 0))
    pages = [pl.BlockSpec((1, 1, D_MODEL, PAGE), lambda bi, j, pt, v=v: (pt[bi, last - n_vis * j - v], layer, 0, 0))
             for v in range(n_vis)]
    grid_spec = pltpu.PrefetchScalarGridSpec(
        num_scalar_prefetch=1,
        grid=(b, n_pages // n_vis),
        in_specs=[seq(n_rows), pl.BlockSpec((n_rows, 1), lambda bi, j, pt: (0, 0)), seq(PAGE), seq(PAGE)]
        + pages + pages + [seq(n_new), pl.BlockSpec((2 * PAGE, PAGE), lambda bi, j, pt: (0, 0))],
        out_specs=seq(n_new),
        scratch_shapes=[pltpu.VMEM((n_rows, D_MODEL), F32), pltpu.VMEM((n_rows, 1), F32)],
    )
    return pl.pallas_call(
        functools.partial(_sb_sample_body, n_new=n_new, n_vis=n_vis),
        grid_spec=grid_spec,
        out_shape=jax.ShapeDtypeStruct((b, n_new, D_MODEL), BF16),
        compiler_params=_cparams("parallel", "arbitrary"),
        name="sb_sample",
    )(page_table, qt, bias_rows, k_new, v_new, *([cache_kt] * n_vis), *([cache_vt] * n_vis), sbz, tri)


def _rw_prep_body(pr_ref, pk_ref, pv_ref, pl_ref, qr_ref, qk_ref, qv_ref, ql_ref,
                  mu_ref, mul_ref, w0_ref, w2_ref, a0_ref, a2_ref, kkg_ref, ka_ref, rk_ref, seg_ref,
                  r_ref, w_ref, k_ref, v_ref, kk_ref, b_ref, bonus_ref):
    def shift(cur_ref, prev_ref, mu):
        cur = cur_ref[...]
        return cur + (prev_ref[...] - cur) * mu

    r = shift(pr_ref, qr_ref, mu_ref[0:1, :])
    k = shift(pk_ref, qk_ref, mu_ref[1:2, :])
    v = shift(pv_ref, qv_ref, mu_ref[2:3, :])
    lo = shift(pl_ref, ql_ref, mul_ref[...])
    lane = lax.broadcasted_iota(jnp.int32, lo.shape, 1)
    wd = jnp.where(lane < LORA, jnp.tanh(lo), 0.0).astype(BF16)
    ad = jnp.where(lane >= LORA, lo, 0.0).astype(BF16)
    log_decay = -_softplus(-(w0_ref[...] + jnp.dot(wd, w2_ref[...], preferred_element_type=F32))) - 0.5
    a = _sigmoid(a0_ref[...] + jnp.dot(ad, a2_ref[...], preferred_element_type=F32))
    kk = k * kkg_ref[...]
    kk = kk * lax.rsqrt(_seg_sum(kk * kk, seg_ref) + KK_EPS)
    k = k * (1.0 + (a - 1.0) * ka_ref[...])
    r_ref[...] = r
    w_ref[...] = -jnp.exp(log_decay)
    k_ref[...] = k
    v_ref[...] = v
    kk_ref[...] = kk
    b_ref[...] = kk * a
    bonus_ref[...] = _seg_sum(r * k * rk_ref[...], seg_ref) * v


def _rw_prep(u, ul, prev, prev_l, mu3, mul, w0, w2p, a0, a2p, kkg, ka, rk, seg):
    m = u.shape[0]
    tm = _row_tile(m, 256)
    col = lambda c: pl.BlockSpec((tm, D_MODEL), lambda i, c=c: (i, c))
    vec = pl.BlockSpec((1, D_MODEL), lambda i: (0, 0))
    lora = pl.BlockSpec((tm, LANES), lambda i: (i, 0))
    lora_w = pl.BlockSpec((LANES, D_MODEL), lambda i: (0, 0))
    out = pl.BlockSpec((tm, D_MODEL), lambda i: (i, 0))
    return pl.pallas_call(
        _rw_prep_body,
        grid=(m // tm,),
        in_specs=[col(C_R), col(C_RK), col(C_RV), lora,
                  col(0), col(1), col(2), lora,
                  pl.BlockSpec((3, D_MODEL), lambda i: (0, 0)), pl.BlockSpec((1, LANES), lambda i: (0, 0)),
                  vec, lora_w, vec, lora_w, vec, vec, vec,
                  pl.BlockSpec((D_MODEL, D_MODEL), lambda i: (0, 0))],
        out_specs=[out] * 7,
        out_shape=[jax.ShapeDtypeStruct((m, D_MODEL), F32)] * 7,
        compiler_params=_cparams("parallel"),
        name="rw_prep",
    )(u, u, u, ul, prev, prev, prev, prev_l, mu3, mul, w0, w2p, a0, a2p, kkg, ka, rk, seg)


def _rw_scan_body(r_ref, w_ref, k_ref, v_ref, kk_ref, b_ref, s0_ref, y_ref, sT_ref,
                  s_ref, vt_ref, yt_ref, *, n_steps):
    c = pl.program_id(1)
    tc = r_ref.shape[1]

    @pl.when(c == 0)
    def _():
        s_ref[...] = s0_ref[0]

    yt_ref[...] = jnp.zeros_like(yt_ref)
    for p in range(N_PAIRS):
        vt_ref[p] = v_ref[0, :, LANES * p:LANES * (p + 1)].T

    lane_s = lax.broadcasted_iota(jnp.int32, (HEAD_DIM, LANES), 1)
    lane_t = lax.broadcasted_iota(jnp.int32, (HEAD_DIM, tc), 1)
    even = lane_s < HEAD_DIM

    def pair_sum(x):
        se = jnp.sum(jnp.where(even, x, 0.0), axis=1, keepdims=True)
        so = jnp.sum(jnp.where(even, 0.0, x), axis=1, keepdims=True)
        return se, so

    def group(g, carry):
        t0 = pl.multiple_of(g * SUBLANES, SUBLANES)
        for p in range(N_PAIRS):
            sl = slice(LANES * p, LANES * (p + 1))
            rows = [ref[0, pl.ds(t0, SUBLANES), sl] for ref in (kk_ref, w_ref, b_ref, k_ref, r_ref)]
            rows[1] = jnp.exp(rows[1])
            s = s_ref[p]
            for i in range(min(SUBLANES, n_steps)):
                kk_t, w_t, b_t, k_t, r_t = (x[i:i + 1, :] for x in rows)
                hot = lane_t == t0 + i
                sa_e, sa_o = pair_sum(s * kk_t)
                sa = jnp.where(even, sa_e, sa_o)
                v_e = jnp.sum(jnp.where(hot, vt_ref[p, 0:HEAD_DIM, :], 0.0), axis=1, keepdims=True)
                v_o = jnp.sum(jnp.where(hot, vt_ref[p, HEAD_DIM:LANES, :], 0.0), axis=1, keepdims=True)
                vc = jnp.where(even, v_e, v_o)
                s = s * w_t - sa * b_t + vc * k_t
                y_e, y_o = pair_sum(s * r_t)
                yt_ref[p, 0:HEAD_DIM, :] = jnp.where(hot, y_e, yt_ref[p, 0:HEAD_DIM, :])
                yt_ref[p, HEAD_DIM:LANES, :] = jnp.where(hot, y_o, yt_ref[p, HEAD_DIM:LANES, :])
            s_ref[p] = s
        return carry

    lax.fori_loop(0, -(-n_steps // SUBLANES), group, 0)

    for p in range(N_PAIRS):
        y_ref[0, :, LANES * p:LANES * (p + 1)] = yt_ref[p].T

    @pl.when(c == pl.num_programs(1) - 1)
    def _():
        sT_ref[0] = s_ref[...]


def _rw_scan(r, w, k, v, kk, b, s0, n_valid):
    bsz, t, _ = r.shape
    tc = LANES
    assert t % tc == 0 and (n_valid == tc or t == tc)
    assert n_valid % SUBLANES == 0 or n_valid < SUBLANES
    seq = pl.BlockSpec((1, tc, D_MODEL), lambda bi, c: (bi, c, 0))
    st = pl.BlockSpec((1, N_PAIRS, HEAD_DIM, LANES), lambda bi, c: (bi, 0, 0, 0))
    return pl.pallas_call(
        functools.partial(_rw_scan_body, n_steps=n_valid),
        grid=(bsz, t // tc),
        in_specs=[seq] * 6 + [st],
        out_specs=[seq, st],
        out_shape=[jax.ShapeDtypeStruct((bsz, t, D_MODEL), F32),
                   jax.ShapeDtypeStruct((bsz, N_PAIRS, HEAD_DIM, LANES), F32)],
        scratch_shapes=[pltpu.VMEM((N_PAIRS, HEAD_DIM, LANES), F32),
                        pltpu.VMEM((N_PAIRS, LANES, tc), F32),
                        pltpu.VMEM((N_PAIRS, LANES, tc), F32)],
        compiler_params=_cparams("parallel", "arbitrary"),
        name="rw_scan",
    )(r, w, k, v, kk, b, s0)


CHUNK = 64


def _dot3(a, b, dims=_NN):
    ah, al = _split_bf16(a)
    bh, bl = _split_bf16(b)
    d = lambda x, y: lax.dot_general(x, y, dims, preferred_element_type=F32)
    return d(ah, bh) + d(ah, bl) + d(al, bh)


def _unit_lower_inverse(mats, n_head):
    n = mats[0].shape[0]
    r = lax.broadcasted_iota(jnp.int32, (n, n), 0)
    c = lax.broadcasted_iota(jnp.int32, (n, n), 1)
    same = lambda m: (r // m) == (c // m)
    eye = jnp.where(r == c, 1.0, 0.0)
    nb = [jnp.where(same(SUBLANES), a, 0.0) for a in mats]
    nb2 = [_dot3(x, x) for x in nb]
    inv = [eye - x for x in nb]
    inv = [t + _dot3(t, x2) for t, x2 in zip(inv, nb2)]
    nb4 = [_dot3(x2, x2) for x2 in nb2]
    inv = [t + _dot3(t, x4) for t, x4 in zip(inv, nb4)]
    m = SUBLANES
    while m < n_head:
        sel = same(2 * m) & jnp.logical_not(same(m))
        prod = [_dot3(jnp.where(sel, a, 0.0), t) for a, t in zip(mats, inv)]
        inv = [t - _dot3(t, x) for t, x in zip(inv, prod)]
        m *= 2
    return inv


def _rw_chunk_prep_body(r_ref, lw_ref, k_ref, v_ref, kk_ref, b_ref, tri_ref, w3_ref, w4_ref, g_ref, h_ref):
    cs = r_ref.shape[1]
    lw = lw_ref[0]
    hi, lo = _split_bf16(lw)
    tri = tri_ref[...]
    cum = jnp.dot(tri, hi, preferred_element_type=F32) + jnp.dot(tri, lo, preferred_element_type=F32)
    cum_end = cum[cs - 1:cs, :]
    grow = jnp.exp(-cum)
    to_end = jnp.exp(cum_end - cum)
    k = k_ref[0]
    b = b_ref[0]
    kt_all = k * grow
    bt_all = b * grow
    kd_all = k * to_end
    bd_all = b * to_end
    kkh_all = kk_ref[0] * jnp.exp(cum - lw)
    rh_all = r_ref[0] * jnp.exp(cum)
    dend_all = jnp.exp(cum_end)
    v_all = v_ref[0]

    n = 2 * cs
    lane = lax.broadcasted_iota(jnp.int32, (cs, LANES), 1)
    even = lane < HEAD_DIM
    row = lax.broadcasted_iota(jnp.int32, (n, n), 0)
    col = lax.broadcasted_iota(jnp.int32, (n, n), 1)
    same_head = (row // cs) == (col // cs)
    strict = same_head & ((col % cs) < (row % cs))
    incl = same_head & ((col % cs) <= (row % cs))
    diag = lax.broadcasted_iota(jnp.int32, (LANES, LANES), 0) == lax.broadcasted_iota(jnp.int32, (LANES, LANES), 1)
    pairs = range(N_PAIRS)
    sl = [slice(LANES * p, LANES * (p + 1)) for p in pairs]

    def embed(x):
        return [jnp.concatenate([jnp.where(even, x[:, s], 0.0), jnp.where(even, 0.0, x[:, s])], axis=0) for s in sl]

    kt, bt, kd, bd, kkh, rh, v = (embed(x) for x in (kt_all, bt_all, kd_all, bd_all, kkh_all, rh_all, v_all))
    a = [_dot3(jnp.concatenate([kkh[p], rh[p]], axis=0), jnp.concatenate([kt[p], bt[p]], axis=0), _NT)
         for p in pairs]
    a_kk = [jnp.where(strict, x[0:n, 0:n], 0.0) for x in a]
    a_kb = [jnp.where(strict, x[0:n, n:2 * n], 0.0) for x in a]
    a_rk = [jnp.where(incl, x[n:2 * n, 0:n], 0.0) for x in a]
    a_rb = [jnp.where(incl, x[n:2 * n, n:2 * n], 0.0) for x in a]
    pv = [_dot3(jnp.concatenate([a_kk[p], a_rk[p]], axis=0), v[p]) for p in pairs]
    hkv = [_dot3(kd[p], v[p], _TN) for p in pairs]
    inv = _unit_lower_inverse(a_kb, cs)
    w12 = [_dot3(inv[p], jnp.concatenate([kkh[p], pv[p][0:n]], axis=1)) for p in pairs]
    q12 = [_dot3(a_rb[p], w12[p]) for p in pairs]
    e12 = [_dot3(bd[p], w12[p], _TN) for p in pairs]
    for p in pairs:
        w3 = rh[p] - q12[p][:, 0:LANES]
        w4 = pv[p][n:2 * n] - q12[p][:, LANES:2 * LANES]
        w3_ref[0, :, sl[p]] = w3[0:cs] + w3[cs:n]
        w4_ref[0, :, sl[p]] = w4[0:cs] + w4[cs:n]
        g_ref[0, 0, p] = jnp.where(diag, dend_all[:, sl[p]], 0.0) - e12[p][:, 0:LANES]
        h_ref[0, 0, p] = hkv[p] - e12[p][:, LANES:2 * LANES]


def _rw_chunk_prep(r, lw, k, v, kk, b):
    bsz, t, _ = r.shape
    nc = t // CHUNK
    idx = jnp.arange(CHUNK)
    tri = (idx[None, :] <= idx[:, None]).astype(BF16)
    seq = pl.BlockSpec((1, CHUNK, D_MODEL), lambda bi, c: (bi, c, 0))
    mat = pl.BlockSpec((1, 1, N_PAIRS, LANES, LANES), lambda bi, c: (bi, c, 0, 0, 0))
    mat_shape = jax.ShapeDtypeStruct((bsz, nc, N_PAIRS, LANES, LANES), F32)
    return pl.pallas_call(
        _rw_chunk_prep_body,
        grid=(bsz, nc),
        in_specs=[seq] * 6 + [pl.BlockSpec((CHUNK, CHUNK), lambda bi, c: (0, 0))],
        out_specs=[seq, seq, mat, mat],
        out_shape=[jax.ShapeDtypeStruct((bsz, t, D_MODEL), F32)] * 2 + [mat_shape] * 2,
        compiler_params=_cparams("parallel", "parallel"),
        name="rw_chunk_prep",
    )(r, lw, k, v, kk, b, tri)


def _rw_chunk_scan_body(w3_ref, w4_ref, g_ref, h_ref, z0_ref, y_ref, zT_ref, z_ref):
    c = pl.program_id(1)
    cs = w3_ref.shape[1]

    @pl.when(c == 0)
    def _():
        z_ref[...] = z0_ref[0]

    for p in range(N_PAIRS):
        sl = slice(LANES * p, LANES * (p + 1))
        out = _dot3(jnp.concatenate([w3_ref[0, :, sl], g_ref[0, 0, p]], axis=0), z_ref[p])
        y_ref[0, :, sl] = out[0:cs] + w4_ref[0, :, sl]
        z_ref[p] = out[cs:cs + LANES] + h_ref[0, 0, p]

    @pl.when(c == pl.num_programs(1) - 1)
    def _():
        zT_ref[0] = z_ref[...]


def _rw_chunk_scan(w3, w4, g, h, z0):
    bsz, t, _ = w3.shape
    seq = pl.BlockSpec((1, CHUNK, D_MODEL), lambda bi, c: (bi, c, 0))
    mat = pl.BlockSpec((1, 1, N_PAIRS, LANES, LANES), lambda bi, c: (bi, c, 0, 0, 0))
    st = pl.BlockSpec((1, N_PAIRS, LANES, LANES), lambda bi, c: (bi, 0, 0, 0))
    return pl.pallas_call(
        _rw_chunk_scan_body,
        grid=(bsz, t // CHUNK),
        in_specs=[seq, seq, mat, mat, st],
        out_specs=[seq, st],
        out_shape=[jax.ShapeDtypeStruct((bsz, t, D_MODEL), F32),
                   jax.ShapeDtypeStruct((bsz, N_PAIRS, LANES, LANES), F32)],
        scratch_shapes=[pltpu.VMEM((N_PAIRS, LANES, LANES), F32)],
        compiler_params=_cparams("parallel", "arbitrary"),
        name="rw_chunk_scan",
    )(w3, w4, g, h, z0)


def _state_to_blockdiag(s):
    b = s.shape[0]
    st = jnp.swapaxes(s, 2, 3).reshape(b, N_PAIRS, 2, HEAD_DIM, HEAD_DIM)
    z = jnp.zeros((b, N_PAIRS, 2, HEAD_DIM, 2, HEAD_DIM), F32)
    z = z.at[:, :, 0, :, 0, :].set(st[:, :, 0]).at[:, :, 1, :, 1, :].set(st[:, :, 1])
    return z.reshape(b, N_PAIRS, LANES, LANES)


def _blockdiag_to_state(z):
    b = z.shape[0]
    z6 = z.reshape(b, N_PAIRS, 2, HEAD_DIM, 2, HEAD_DIM)
    st = jnp.stack([z6[:, :, 0, :, 0, :], z6[:, :, 1, :, 1, :]], axis=2)
    return jnp.swapaxes(st, 3, 4).reshape(b, N_HEADS, HEAD_DIM, HEAD_DIM)


def _rw_post_body(y_ref, bonus_ref, z_ref, g_ref, b_ref, seg_ref, o_ref):
    y = y_ref[...]
    mu = _seg_sum(y, seg_ref) * (1.0 / HEAD_DIM)
    d = y - mu
    var = _seg_sum(d * d, seg_ref) * (1.0 / HEAD_DIM)
    yn = d * lax.rsqrt(var + LNX_EPS) * g_ref[...] + b_ref[...]
    o_ref[...] = ((yn + bonus_ref[...]) * _silu(z_ref[...])).astype(BF16)


def _rw_post(y, bonus, u, g, b, seg):
    m = y.shape[0]
    tm = _row_tile(m, 512)
    row = pl.BlockSpec((tm, D_MODEL), lambda i: (i, 0))
    vec = pl.BlockSpec((1, D_MODEL), lambda i: (0, 0))
    return pl.pallas_call(
        _rw_post_body,
        grid=(m // tm,),
        in_specs=[row, row, pl.BlockSpec((tm, D_MODEL), lambda i: (i, C_RZ)), vec, vec,
                  pl.BlockSpec((D_MODEL, D_MODEL), lambda i: (0, 0))],
        out_specs=row,
        out_shape=jax.ShapeDtypeStruct((m, D_MODEL), BF16),
        compiler_params=_cparams("parallel"),
        name="rw_post",
    )(y, bonus, u, g, b, seg)


def _merge_body(x_ref, oa_ref, ob_ref, oc_ref, g0_ref, g1_ref, g2_ref,
                wa_ref, wb_ref, wc_ref, wo_ref, o_ref):
    merged = (_sigmoid(g0_ref[...]) * jnp.dot(oa_ref[...], wa_ref[...], preferred_element_type=F32)
              + _sigmoid(g1_ref[...]) * jnp.dot(ob_ref[...], wb_ref[...], preferred_element_type=F32)
              + _sigmoid(g2_ref[...]) * jnp.dot(oc_ref[...], wc_ref[...], preferred_element_type=F32))
    o_ref[...] = x_ref[...] + jnp.dot(merged.astype(BF16), wo_ref[...], preferred_element_type=F32)


def _merge(x, oa, ob, oc, u, wa, wb, wc, wo):
    m = x.shape[0]
    tm = _row_tile(m, 512)
    row = pl.BlockSpec((tm, D_MODEL), lambda i: (i, 0))
    col = lambda c: pl.BlockSpec((tm, D_MODEL), lambda i, c=c: (i, c))
    wsp = pl.BlockSpec((D_MODEL, D_MODEL), lambda i: (0, 0))
    return pl.pallas_call(
        _merge_body,
        grid=(m // tm,),
        in_specs=[row, row, row, row, col(C_G0), col(C_G1), col(C_G2), wsp, wsp, wsp, wsp],
        out_specs=row,
        out_shape=jax.ShapeDtypeStruct((m, D_MODEL), F32),
        compiler_params=_cparams("parallel"),
        name="merge",
    )(x, oa, ob, oc, u, u, u, wa, wb, wc, wo)


def _later_matrix(n, lhs):
    i = jnp.arange(n)
    m = (i[:, None] > i[None, :]) if not lhs else (i[None, :] > i[:, None])
    return m.astype(BF16)


def _pack_pairs(s):
    b = s.shape[0]
    return s.reshape(b, N_PAIRS, 2, HEAD_DIM, HEAD_DIM).transpose(0, 1, 3, 2, 4).reshape(b, N_PAIRS, HEAD_DIM, LANES)


def _unpack_pairs(s):
    b = s.shape[0]
    return s.reshape(b, N_PAIRS, HEAD_DIM, 2, HEAD_DIM).transpose(0, 1, 3, 2, 4).reshape(b, N_HEADS, HEAD_DIM, HEAD_DIM)


def _layer(x, conv_prev, shift_prev, wkv_prev, lw, attend):
    b, t, _ = x.shape
    m = b * t
    x2 = x.reshape(m, D_MODEL)
    u = _in_proj(x2, lw['norm_g'], lw['w_main'], 1024)
    ul = _in_proj(x2, lw['norm_g'], lw['w_lora'], LANES)

    glu, qb, kf, kb, vb = _pre(u, lw['q_g'], lw['k_g'], lw['seg'])
    u3 = u.reshape(b, t, N_MAIN)

    glu3 = glu.reshape(b, t, D_MODEL)
    hist = jnp.concatenate([jnp.zeros((b, CONV_HALO - (CONV_W - 1), D_MODEL), F32), conv_prev], axis=1)
    ext = jnp.concatenate([hist, glu3], axis=1)
    oa = _conv(glu3, ext, u3, lw['conv_w'], lw['conv_b'], lw['conv_ln_g'], lw['conv_ln_b'])
    conv_new = ext[:, -(CONV_W - 1):]

    ob = attend(qb.reshape(b, t, D_MODEL), kb.reshape(b, t, D_MODEL), vb.reshape(b, t, D_MODEL), u3)
    v_out = u3[:, :, C_V * D_MODEL:(C_V + 1) * D_MODEL]

    p_main = u3[:, :, C_R * D_MODEL:(C_RV + 1) * D_MODEL]
    ul3 = ul.reshape(b, t, LANES)
    prev = jnp.concatenate([shift_prev[:, None, :3 * D_MODEL], p_main[:, :-1]], axis=1).reshape(m, 3 * D_MODEL)
    prev_l = jnp.concatenate([shift_prev[:, None, 3 * D_MODEL:], ul3[:, :-1]], axis=1).reshape(m, LANES)
    r, w, k, v, kk, bb, bonus = _rw_prep(u, ul, prev, prev_l, lw['mu3'], lw['mul'], lw['w0'], lw['w2p'],
                                         lw['a0'], lw['a2p'], lw['k_k'], lw['k_a'], lw['r_k'], lw['seg'])
    seqs = [a.reshape(b, t, D_MODEL) for a in (r, w, k, v, kk, bb)]
    if t % CHUNK == 0:
        w3, w4, g, h = _rw_chunk_prep(*seqs)
        y, z_new = _rw_chunk_scan(w3, w4, g, h, _state_to_blockdiag(wkv_prev))
        s_new = _blockdiag_to_state(z_new)
    else:
        assert t < SUBLANES
        seqs = [jnp.pad(a, ((0, 0), (0, LANES - t), (0, 0))) for a in seqs]
        y, s_new = _rw_scan(*seqs, _pack_pairs(wkv_prev), t)
        y, s_new = y[:, :t], _unpack_pairs(s_new)
    y = y.reshape(m, D_MODEL)
    oc = _rw_post(y, bonus, u, lw['lnx_g'], lw['lnx_b'], lw['seg'])
    shift_new = jnp.concatenate([p_main[:, -1], ul3[:, -1]], axis=-1)

    x_new = _merge(x2, oa.reshape(m, D_MODEL), ob.reshape(m, D_MODEL), oc, u,
                   lw['w_pa'], lw['w_pb'], lw['w_pc'], lw['w_out'])
    return (x_new.reshape(b, t, D_MODEL), kf.reshape(b, t, N_HEADS, HEAD_DIM),
            v_out.reshape(b, t, N_HEADS, HEAD_DIM), s_new, shift_new, conv_new)


def kernel(x_prompt, x_sample, cache_k, cache_v, state_wkv, state_shift, state_conv, page_table, norm_g, w_in,
           conv_w, conv_b, conv_ln_g, conv_ln_b, w_pa, q_norm_g, k_norm_g, sb_bias, w_pb, mu_shift, w0, w2, a0,
           a2, k_k, k_a, r_k, lnx_g, lnx_b, w_pc, w_out):
    depth = w_in.shape[0]
    nb_p, t_p, _ = x_prompt.shape
    nb_s, t_s, _ = x_sample.shape
    n_pool = cache_k.shape[0]
    lora0 = 10 * D_MODEL
    seg = jnp.kron(jnp.eye(N_HEADS, dtype=F32), jnp.ones((HEAD_DIM, HEAD_DIM), F32)).astype(BF16)
    vec = lambda a: a.reshape(1, -1)
    cache_kt = jnp.transpose(cache_k, (0, 1, 3, 4, 2)).reshape(n_pool, depth, D_MODEL, PAGE)
    cache_vt = jnp.transpose(cache_v, (0, 1, 3, 4, 2)).reshape(n_pool, depth, D_MODEL, PAGE)
    assert t_s <= PAGE
    tri_prompt = _later_matrix(SB_TK, lhs=False)
    tri_page = jnp.tile(_later_matrix(PAGE, lhs=False), (2, 1))
    qrow_head = jnp.arange(N_HEADS * t_s) // t_s
    lane_head = jnp.arange(D_MODEL) // HEAD_DIM

    x_p, x_s = x_prompt, x_sample
    outs_p = [[] for _ in range(5)]
    outs_s = [[] for _ in range(5)]
    for l in range(depth):
        wl = w_in[l]
        zpad = jnp.zeros((LORA, D_MODEL), F32)
        lw = {
            'norm_g': vec(norm_g[l]),
            'w_main': jnp.concatenate([wl[:, :lora0], wl[:, lora0 + 2 * LORA:]], axis=1).astype(BF16),
            'w_lora': wl[:, lora0:lora0 + 2 * LORA].astype(BF16),
            'q_g': vec(jnp.tile(q_norm_g[l], N_HEADS)), 'k_g': vec(jnp.tile(k_norm_g[l], N_HEADS)),
            'seg': seg,
            'conv_w': jnp.concatenate([conv_w[l], jnp.zeros((CONV_HALO - CONV_W, D_MODEL), F32)], axis=0),
            'conv_b': vec(conv_b[l]), 'conv_ln_g': vec(conv_ln_g[l]), 'conv_ln_b': vec(conv_ln_b[l]),
            'mu3': mu_shift[l, :3 * D_MODEL].reshape(3, D_MODEL), 'mul': vec(mu_shift[l, 3 * D_MODEL:]),
            'w0': vec(w0[l]), 'w2p': jnp.concatenate([w2[l], zpad], axis=0).astype(BF16),
            'a0': vec(a0[l]), 'a2p': jnp.concatenate([zpad, a2[l]], axis=0).astype(BF16),
            'k_k': vec(k_k[l]), 'k_a': vec(k_a[l]), 'r_k': vec(r_k[l]),
            'lnx_g': vec(lnx_g[l]), 'lnx_b': vec(lnx_b[l]),
            'w_pa': w_pa[l].astype(BF16), 'w_pb': w_pb[l].astype(BF16), 'w_pc': w_pc[l].astype(BF16),
            'w_out': w_out[l].astype(BF16),
        }
        bias_l = sb_bias[l]

        def attend_prompt(qb, kb, vb, u3):
            return _sb_prompt(qb, kb, vb, u3, bias_l, tri_prompt, SB_TQ, SB_TK, SB_SPLIT)

        def attend_sample(qb, kb, vb, u3):
            qt = jnp.tile(qb, (1, N_HEADS, 1))
            qt = jnp.where(qrow_head[:, None] == lane_head[None, :], qt, jnp.zeros_like(qt))
            pad = ((0, 0), (0, PAGE - t_s), (0, 0))
            sbz = u3[:, :, C_SBZ * D_MODEL:(C_SBZ + 1) * D_MODEL]
            return _sb_sample(page_table, qt, jnp.repeat(bias_l, t_s)[:, None], jnp.pad(kb, pad), jnp.pad(vb, pad),
                              cache_kt, cache_vt, l, sbz, tri_page)

        res_p = _layer(x_p, jnp.zeros((nb_p, CONV_W - 1, D_MODEL), F32), jnp.zeros((nb_p, state_shift.shape[2]), F32),
                       jnp.zeros((nb_p,) + state_wkv.shape[2:], F32), lw, attend_prompt)
        res_s = _layer(x_s, state_conv[:, l], state_shift[:, l], state_wkv[:, l], lw, attend_sample)
        x_p, x_s = res_p[0], res_s[0]
        for i in range(5):
            outs_p[i].append(res_p[i + 1])
            outs_s[i].append(res_s[i + 1])
    stack = lambda xs: jnp.stack(xs, axis=1)
    return (x_p, x_s) + tuple(stack(o) for o in outs_p) + tuple(stack(o) for o in outs_s)
```

```python
import functools

import jax
import jax.numpy as jnp
from jax import lax
from jax.experimental import pallas as pl
from jax.experimental.pallas import tpu as pltpu

F32 = jnp.float32
BF16 = jnp.bfloat16

D_MODEL = 1024
N_HEADS = 16
HEAD_DIM = 64
N_PAIRS = N_HEADS // 2
LANES = 128
SUBLANES = 8
CONV_W = 31
CONV_HALO = 32
LORA = 64
PAGE = 128
NORM_EPS = 1e-6
CONV_LN_EPS = 1e-5
LNX_EPS = 64e-5
KK_EPS = 1e-12
SB_SCALE = HEAD_DIM ** -0.5
LOG2E = 1.4426950408889634
_NT = (((1,), (1,)), ((), ()))
_TN = (((0,), (0,)), ((), ()))
_NN = (((1,), (0,)), ((), ()))

C_VAL, C_GATE, C_Z, C_Q, C_K, C_V, C_SBZ, C_R, C_RK, C_RV, C_RZ, C_G0, C_G1, C_G2 = range(14)
N_MAIN = 14 * D_MODEL

VMEM_LIMIT = 48 * 1024 * 1024
SB_TQ, SB_TK, SB_SPLIT = 256, 256, 1
SLAB = 32


def _cparams(*sem):
    return pltpu.CompilerParams(dimension_semantics=sem, vmem_limit_bytes=VMEM_LIMIT)


def _sigmoid(x):
    return 1.0 / (1.0 + jnp.exp(-x))


def _silu(x):
    return x * _sigmoid(x)


def _softplus(x):
    return jnp.maximum(x, 0.0) + jnp.log(1.0 + jnp.exp(-jnp.abs(x)))


def _split_bf16(x):
    hi = x.astype(BF16)
    lo = (x - hi.astype(F32)).astype(BF16)
    return hi, lo


def _seg_sum(x, seg_ref):
    hi, lo = _split_bf16(x)
    seg = seg_ref[...]
    return (jnp.dot(hi, seg, preferred_element_type=F32)
            + jnp.dot(lo, seg, preferred_element_type=F32))


def _row_tile(m, pref):
    t = min(m, pref)
    assert m % t == 0
    return t


def _in_proj_body(x_ref, g_ref, w_ref, o_ref, h_ref):
    @pl.when(pl.program_id(1) == 0)
    def _():
        x = x_ref[...]
        ms = jnp.mean(x * x, axis=-1, keepdims=True)
        h_ref[...] = (x * lax.rsqrt(ms + NORM_EPS) * g_ref[...]).astype(BF16)

    o_ref[...] = jnp.dot(h_ref[...], w_ref[...], preferred_element_type=F32)


def _in_proj(x, g, w, tn):
    m, n = x.shape[0], w.shape[1]
    tm = _row_tile(m, 1024)
    return pl.pallas_call(
        _in_proj_body,
        grid=(m // tm, n // tn),
        in_specs=[pl.BlockSpec((tm, D_MODEL), lambda i, j: (i, 0)),
                  pl.BlockSpec((1, D_MODEL), lambda i, j: (0, 0)),
                  pl.BlockSpec((D_MODEL, tn), lambda i, j: (0, j))],
        out_specs=pl.BlockSpec((tm, tn), lambda i, j: (i, j)),
        out_shape=jax.ShapeDtypeStruct((m, n), F32),
        scratch_shapes=[pltpu.VMEM((tm, D_MODEL), BF16)],
        compiler_params=_cparams("parallel", "arbitrary"),
        name="in_proj",
    )(x, g, w)


def _pre_body(cv_ref, cg_ref, q_ref, k_ref, v_ref, qg_ref, kg_ref, seg_ref,
              glu_ref, qb_ref, kf_ref, kb_ref, vb_ref):
    glu_ref[...] = cv_ref[...] * _sigmoid(cg_ref[...])
    q = q_ref[...]
    k = k_ref[...]
    q_ms = _seg_sum(q * q, seg_ref) * (1.0 / HEAD_DIM)
    k_ms = _seg_sum(k * k, seg_ref) * (1.0 / HEAD_DIM)
    qn = q * lax.rsqrt(q_ms + NORM_EPS) * qg_ref[...]
    kn = k * lax.rsqrt(k_ms + NORM_EPS) * kg_ref[...]
    qb_ref[...] = (qn * (SB_SCALE * LOG2E)).astype(BF16)
    kf_ref[...] = kn
    kb_ref[...] = kn.astype(BF16)
    vb_ref[...] = v_ref[...].astype(BF16)


def _pre(u, qg, kg, seg):
    m = u.shape[0]
    tm = _row_tile(m, 512)
    col = lambda c: pl.BlockSpec((tm, D_MODEL), lambda i, c=c: (i, c))
    vec = pl.BlockSpec((1, D_MODEL), lambda i: (0, 0))
    out = pl.BlockSpec((tm, D_MODEL), lambda i: (i, 0))
    sds = lambda dt: jax.ShapeDtypeStruct((m, D_MODEL), dt)
    return pl.pallas_call(
        _pre_body,
        grid=(m // tm,),
        in_specs=[col(C_VAL), col(C_GATE), col(C_Q), col(C_K), col(C_V), vec, vec,
                  pl.BlockSpec((D_MODEL, D_MODEL), lambda i: (0, 0))],
        out_specs=[out] * 5,
        out_shape=[sds(F32), sds(BF16), sds(F32), sds(BF16), sds(BF16)],
        compiler_params=_cparams("parallel"),
        name="pre",
    )(u, u, u, u, u, qg, kg, seg)


def _conv_body(cur_ref, halo_ref, cz_ref, cw_ref, cb_ref, lg_ref, lb_ref, o_ref, ext_ref):
    tt = cur_ref.shape[1]
    ext_ref[0:CONV_HALO, :] = halo_ref[0]
    ext_ref[CONV_HALO:CONV_HALO + tt, :] = cur_ref[0]
    first = CONV_HALO - (CONV_W - 1)
    acc = jnp.zeros((tt, D_MODEL), F32)
    for w in range(CONV_W):
        acc = acc + cw_ref[w:w + 1, :] * ext_ref[first + w:first + w + tt, :]
    c = acc + cb_ref[...]
    mu = jnp.mean(c, axis=-1, keepdims=True)
    d = c - mu
    var = jnp.mean(d * d, axis=-1, keepdims=True)
    cn = d * lax.rsqrt(var + CONV_LN_EPS) * lg_ref[...] + lb_ref[...]
    o_ref[0] = (_silu(cn) * _silu(cz_ref[0])).astype(BF16)


def _conv(glu, ext, u, cw, cb, lg, lb):
    b, t, _ = glu.shape
    tt = _row_tile(t, 128)
    assert tt % CONV_HALO == 0 or t == tt
    hb = max(tt // CONV_HALO, 1)
    vec = pl.BlockSpec((1, D_MODEL), lambda bi, i: (0, 0))
    return pl.pallas_call(
        _conv_body,
        grid=(b, t // tt),
        in_specs=[pl.BlockSpec((1, tt, D_MODEL), lambda bi, i: (bi, i, 0)),
                  pl.BlockSpec((1, CONV_HALO, D_MODEL), lambda bi, i: (bi, i * hb, 0)),
                  pl.BlockSpec((1, tt, D_MODEL), lambda bi, i: (bi, i, C_Z)),
                  pl.BlockSpec((CONV_HALO, D_MODEL), lambda bi, i: (0, 0)),
                  vec, vec, vec],
        out_specs=pl.BlockSpec((1, tt, D_MODEL), lambda bi, i: (bi, i, 0)),
        out_shape=jax.ShapeDtypeStruct((b, t, D_MODEL), BF16),
        scratch_shapes=[pltpu.VMEM((CONV_HALO + tt, D_MODEL), F32)],
        compiler_params=_cparams("parallel", "parallel"),
        name="conv",
    )(glu, ext, u, cw, cb, lg, lb)


def _softplus2(x):
    return jnp.maximum(x, 0.0) + jnp.log2(1.0 + jnp.exp2(-jnp.abs(x)))


def _sb_prompt_body(bias_ref, q_ref, k_ref, v_ref, z_ref, tri_ref, o_ref, sp_scr, base_scr, a_scr, first_scr, s_scr,
                    *, tq, tk, n_split):
    p = pl.program_id(1)
    i = pl.program_id(2)
    tr = tq // n_split
    q2 = q_ref[0]
    lane_q = lax.broadcasted_iota(jnp.int32, (tq, LANES), 1)
    lane_k = lax.broadcasted_iota(jnp.int32, (tk, LANES), 1)
    zero_q = jnp.zeros((tq, LANES), BF16)
    zero_k = jnp.zeros((tk, LANES), BF16)
    q_heads = (jnp.where(lane_q < HEAD_DIM, q2, zero_q), jnp.where(lane_q >= HEAD_DIM, q2, zero_q))
    biases = (bias_ref[2 * p] * LOG2E, bias_ref[2 * p + 1] * LOG2E)
    tri = tri_ref[...]
    row = lax.broadcasted_iota(jnp.int32, (SLAB, tk), 0)
    colk = lax.broadcasted_iota(jnp.int32, (SLAB, tk), 1)
    chains = [(h, r) for h in range(2) for r in range(n_split)]
    q_parts = [q_heads[h][r * tr:(r + 1) * tr] for h, r in chains]

    def scores(j, slot):
        kb = k_ref[0, pl.ds(pl.multiple_of(j * tk, tk), tk), :]
        for n, q in enumerate(q_parts):
            s_scr[slot, n] = lax.dot_general(q, kb, _NT, preferred_element_type=F32)

    def weights(slot, cs, j, masked):
        c_out = []
        for n, (c, (h, r)) in enumerate(zip(cs, chains)):
            for r0 in range(0, tr, SLAB):
                rows = slice(r0, r0 + SLAB)
                z = s_scr[slot, n, rows] + biases[h]
                sp = _softplus2(z)
                if masked:
                    sp = jnp.where((colk + j * tk) < (row + (i * tq + r * tr + r0)), sp, 0.0)
                sp_scr[n, rows] = sp.astype(BF16)
                base_scr[n, rows] = z - sp
                first_scr[n, rows] = sp[:, 0:LANES]
            later = jnp.dot(sp_scr[n], tri, preferred_element_type=F32)
            total = later + c
            for r0 in range(0, tr, SLAB):
                rows = slice(r0, r0 + SLAB)
                a = jnp.exp2(base_scr[n, rows] - total[rows])
                if masked:
                    a = jnp.where((colk + j * tk) < (row + (i * tq + r * tr + r0)), a, 0.0)
                a_scr[n, rows] = a.astype(BF16)
            c_out.append(total[:, 0:1] + first_scr[n][:, 0:1])
        return c_out

    def weighted_values(accs, j):
        vb = v_ref[0, pl.ds(pl.multiple_of(j * tk, tk), tk), :]
        v_heads = (jnp.where(lane_k < HEAD_DIM, vb, zero_k), jnp.where(lane_k >= HEAD_DIM, vb, zero_k))
        return [acc + jnp.dot(a_scr[n], v_heads[h], preferred_element_type=F32)
                for n, (acc, (h, r)) in enumerate(zip(accs, chains))]

    cs = [jnp.zeros((tr, 1), F32)] * len(chains)
    accs = [jnp.zeros((tr, LANES), F32)] * len(chains)
    jd = (i * tq) // tk
    scores(jd, 0)
    cs = weights(0, cs, jd, True)
    scores(jnp.maximum(jd - 1, 0), 1)

    def body(it, carry):
        cs, accs = carry
        ja = jd - 1 - 2 * it
        scores(jnp.maximum(ja - 1, 0), 0)
        accs = weighted_values(accs, ja + 1)
        cs = weights(1, cs, ja, False)
        scores(jnp.maximum(ja - 2, 0), 1)
        accs = weighted_values(accs, ja)
        cs = weights(0, cs, ja - 1, False)
        return cs, accs

    cs, accs = lax.fori_loop(0, jd // 2, body, (cs, accs))

    def last_odd(cs, accs):
        accs = weighted_values(accs, 1)
        return weights(1, cs, 0, False), accs

    cs, accs = lax.cond(jd % 2 == 1, last_odd, lambda cs, accs: (cs, accs), cs, accs)
    accs = weighted_values(accs, 0)
    o = jnp.concatenate([accs[r] + accs[n_split + r] for r in range(n_split)], axis=0)
    o_ref[0] = (o * _silu(z_ref[0])).astype(BF16)


def _sb_prompt(qb, kb, vb, u, bias, tri, tq, tk, n_split):
    b, t, _ = qb.shape
    assert t % tk == 0 and tk % tq == 0 and tq % n_split == 0
    return pl.pallas_call(
        functools.partial(_sb_prompt_body, tq=tq, tk=tk, n_split=n_split),
        grid=(b, N_PAIRS, t // tq),
        in_specs=[pl.BlockSpec(memory_space=pltpu.SMEM),
                  pl.BlockSpec((1, tq, LANES), lambda bi, p, i: (bi, i, p)),
                  pl.BlockSpec((1, t, LANES), lambda bi, p, i: (bi, 0, p)),
                  pl.BlockSpec((1, t, LANES), lambda bi, p, i: (bi, 0, p)),
                  pl.BlockSpec((1, tq, LANES), lambda bi, p, i: (bi, i, C_SBZ * N_PAIRS + p)),
                  pl.BlockSpec((tk, tk), lambda bi, p, i: (0, 0))],
        out_specs=pl.BlockSpec((1, tq, LANES), lambda bi, p, i: (bi, i, p)),
        out_shape=jax.ShapeDtypeStruct((b, t, D_MODEL), BF16),
        scratch_shapes=[pltpu.VMEM((2 * n_split, tq // n_split, tk), BF16),
                        pltpu.VMEM((2 * n_split, tq // n_split, tk), F32),
                        pltpu.VMEM((2 * n_split, tq // n_split, tk), BF16),
                        pltpu.VMEM((2 * n_split, tq // n_split, LANES), F32),
                        pltpu.VMEM((2, 2 * n_split, tq // n_split, tk), F32)],
        compiler_params=_cparams("parallel", "parallel", "arbitrary"),
        name="sb_prompt",
    )(bias, qb, kb, vb, u, tri)


def _sb_sample_body(pt_ref, qt_ref, bias_ref, kn_ref, vn_ref, *rest, n_new, n_vis):
    kp_refs, vp_refs = rest[0:n_vis], rest[n_vis:2 * n_vis]
    z_ref, tri_ref, o_ref, acc_ref, c_ref = rest[2 * n_vis:]
    j = pl.program_id(1)
    qt = qt_ref[0]
    bias = bias_ref[...] * LOG2E
    tri = tri_ref[...]

    def visit(zs, vals, vals_dims, mask):
        zs = [z + bias for z in zs]
        sp = [_softplus2(z) for z in zs]
        if mask is not None:
            sp = [jnp.where(mask, x, 0.0) for x in sp]
        later = [jnp.dot(jnp.concatenate(_split_bf16(x), axis=1), tri, preferred_element_type=F32) for x in sp]
        c = c_ref[...]
        acc = acc_ref[...]
        for z, x, lt, v in zip(zs, sp, later, vals):
            a = jnp.exp2(z - x - lt - c)
            if mask is not None:
                a = jnp.where(mask, a, 0.0)
            acc = acc + lax.dot_general(a.astype(BF16), v, vals_dims, preferred_element_type=F32)
            c = c + lt[:, 0:1] + x[:, 0:1]
        acc_ref[...] = acc
        c_ref[...] = c

    @pl.when(j == 0)
    def _():
        acc_ref[...] = jnp.zeros_like(acc_ref)
        c_ref[...] = jnp.zeros_like(c_ref)
        shape = (qt.shape[0], PAGE)
        mask = lax.broadcasted_iota(jnp.int32, shape, 1) < lax.broadcasted_iota(jnp.int32, shape, 0) % n_new
        visit([lax.dot_general(qt, kn_ref[0], _NT, preferred_element_type=F32)], [vn_ref[0]], _NN, mask)

    visit([jnp.dot(qt, r[0, 0].astype(BF16), preferred_element_type=F32) for r in kp_refs],
          [r[0, 0].astype(BF16) for r in vp_refs], _NT, None)

    @pl.when(j == pl.num_programs(1) - 1)
    def _():
        lane = lax.broadcasted_iota(jnp.int32, (n_new, LANES), 1)
        for m in range(N_PAIRS):
            tile = acc_ref[2 * n_new * m:2 * n_new * (m + 1), LANES * m:LANES * (m + 1)]
            o = jnp.where(lane < HEAD_DIM, tile[0:n_new], tile[n_new:2 * n_new])
            zz = z_ref[0, :, LANES * m:LANES * (m + 1)]
            o_ref[0, :, LANES * m:LANES * (m + 1)] = (o * _silu(zz)).astype(BF16)


def _sb_sample(page_table, qt, bias_rows, k_new, v_new, cache_kt, cache_vt, layer, sbz, tri):
    b, n_pages = page_table.shape
    n_new = sbz.shape[1]
    n_rows = qt.shape[1]
    n_vis = max(v for v in (4, 2, 1) if n_pages % v == 0)
    last = n_pages - 1
    seq = lambda rows: pl.BlockSpec((1, rows, D_MODEL), lambda bi, j, pt: (bi, 0, 0))
    pages = [pl.BlockSpec((1, 1, D_MODEL, PAGE), lambda bi, j, pt, v=v: (pt[bi, last - n_vis * j - v], layer, 0, 0))
             for v in range(n_vis)]
    grid_spec = pltpu.PrefetchScalarGridSpec(
        num_scalar_prefetch=1,
        grid=(b, n_pages // n_vis),
        in_specs=[seq(n_rows), pl.BlockSpec((n_rows, 1), lambda bi, j, pt: (0, 0)), seq(PAGE), seq(PAGE)]
        + pages + pages + [seq(n_new), pl.BlockSpec((2 * PAGE, PAGE), lambda bi, j, pt: (0, 0))],
        out_specs=seq(n_new),
        scratch_shapes=[pltpu.VMEM((n_rows, D_MODEL), F32), pltpu.VMEM((n_rows, 1), F32)],
    )
    return pl.pallas_call(
        functools.partial(_sb_sample_body, n_new=n_new, n_vis=n_vis),
        grid_spec=grid_spec,
        out_shape=jax.ShapeDtypeStruct((b, n_new, D_MODEL), BF16),
        compiler_params=_cparams("parallel", "arbitrary"),
        name="sb_sample",
    )(page_table, qt, bias_rows, k_new, v_new, *([cache_kt] * n_vis), *([cache_vt] * n_vis), sbz, tri)


def _rw_prep_body(pr_ref, pk_ref, pv_ref, pl_ref, qr_ref, qk_ref, qv_ref, ql_ref,
                  mu_ref, mul_ref, w0_ref, w2_ref, a0_ref, a2_ref, kkg_ref, ka_ref, rk_ref, seg_ref,
                  r_ref, w_ref, k_ref, v_ref, kk_ref, b_ref, bonus_ref):
    def shift(cur_ref, prev_ref, mu):
        cur = cur_ref[...]
        return cur + (prev_ref[...] - cur) * mu

    r = shift(pr_ref, qr_ref, mu_ref[0:1, :])
    k = shift(pk_ref, qk_ref, mu_ref[1:2, :])
    v = shift(pv_ref, qv_ref, mu_ref[2:3, :])
    lo = shift(pl_ref, ql_ref, mul_ref[...])
    lane = lax.broadcasted_iota(jnp.int32, lo.shape, 1)
    wd = jnp.where(lane < LORA, jnp.tanh(lo), 0.0).astype(BF16)
    ad = jnp.where(lane >= LORA, lo, 0.0).astype(BF16)
    log_decay = -_softplus(-(w0_ref[...] + jnp.dot(wd, w2_ref[...], preferred_element_type=F32))) - 0.5
    a = _sigmoid(a0_ref[...] + jnp.dot(ad, a2_ref[...], preferred_element_type=F32))
    kk = k * kkg_ref[...]
    kk = kk * lax.rsqrt(_seg_sum(kk * kk, seg_ref) + KK_EPS)
    k = k * (1.0 + (a - 1.0) * ka_ref[...])
    r_ref[...] = r
    w_ref[...] = -jnp.exp(log_decay)
    k_ref[...] = k
    v_ref[...] = v
    kk_ref[...] = kk
    b_ref[...] = kk * a
    bonus_ref[...] = _seg_sum(r * k * rk_ref[...], seg_ref) * v


def _rw_prep(u, ul, prev, prev_l, mu3, mul, w0, w2p, a0, a2p, kkg, ka, rk, seg):
    m = u.shape[0]
    tm = _row_tile(m, 256)
    col = lambda c: pl.BlockSpec((tm, D_MODEL), lambda i, c=c: (i, c))
    vec = pl.BlockSpec((1, D_MODEL), lambda i: (0, 0))
    lora = pl.BlockSpec((tm, LANES), lambda i: (i, 0))
    lora_w = pl.BlockSpec((LANES, D_MODEL), lambda i: (0, 0))
    out = pl.BlockSpec((tm, D_MODEL), lambda i: (i, 0))
    return pl.pallas_call(
        _rw_prep_body,
        grid=(m // tm,),
        in_specs=[col(C_R), col(C_RK), col(C_RV), lora,
                  col(0), col(1), col(2), lora,
                  pl.BlockSpec((3, D_MODEL), lambda i: (0, 0)), pl.BlockSpec((1, LANES), lambda i: (0, 0)),
                  vec, lora_w, vec, lora_w, vec, vec, vec,
                  pl.BlockSpec((D_MODEL, D_MODEL), lambda i: (0, 0))],
        out_specs=[out] * 7,
        out_shape=[jax.ShapeDtypeStruct((m, D_MODEL), F32)] * 7,
        compiler_params=_cparams("parallel"),
        name="rw_prep",
    )(u, u, u, ul, prev, prev, prev, prev_l, mu3, mul, w0, w2p, a0, a2p, kkg, ka, rk, seg)


def _rw_scan_body(r_ref, w_ref, k_ref, v_ref, kk_ref, b_ref, s0_ref, y_ref, sT_ref,
                  s_ref, vt_ref, yt_ref, *, n_steps):
    c = pl.program_id(1)
    tc = r_ref.shape[1]

    @pl.when(c == 0)
    def _():
        s_ref[...] = s0_ref[0]

    yt_ref[...] = jnp.zeros_like(yt_ref)
    for p in range(N_PAIRS):
        vt_ref[p] = v_ref[0, :, LANES * p:LANES * (p + 1)].T

    lane_s = lax.broadcasted_iota(jnp.int32, (HEAD_DIM, LANES), 1)
    lane_t = lax.broadcasted_iota(jnp.int32, (HEAD_DIM, tc), 1)
    even = lane_s < HEAD_DIM

    def pair_sum(x):
        se = jnp.sum(jnp.where(even, x, 0.0), axis=1, keepdims=True)
        so = jnp.sum(jnp.where(even, 0.0, x), axis=1, keepdims=True)
        return se, so

    def group(g, carry):
        t0 = pl.multiple_of(g * SUBLANES, SUBLANES)
        for p in range(N_PAIRS):
            sl = slice(LANES * p, LANES * (p + 1))
            rows = [ref[0, pl.ds(t0, SUBLANES), sl] for ref in (kk_ref, w_ref, b_ref, k_ref, r_ref)]
            rows[1] = jnp.exp(rows[1])
            s = s_ref[p]
            for i in range(min(SUBLANES, n_steps)):
                kk_t, w_t, b_t, k_t, r_t = (x[i:i + 1, :] for x in rows)
                hot = lane_t == t0 + i
                sa_e, sa_o = pair_sum(s * kk_t)
                sa = jnp.where(even, sa_e, sa_o)
                v_e = jnp.sum(jnp.where(hot, vt_ref[p, 0:HEAD_DIM, :], 0.0), axis=1, keepdims=True)
                v_o = jnp.sum(jnp.where(hot, vt_ref[p, HEAD_DIM:LANES, :], 0.0), axis=1, keepdims=True)
                vc = jnp.where(even, v_e, v_o)
                s = s * w_t - sa * b_t + vc * k_t
                y_e, y_o = pair_sum(s * r_t)
                yt_ref[p, 0:HEAD_DIM, :] = jnp.where(hot, y_e, yt_ref[p, 0:HEAD_DIM, :])
                yt_ref[p, HEAD_DIM:LANES, :] = jnp.where(hot, y_o, yt_ref[p, HEAD_DIM:LANES, :])
            s_ref[p] = s
        return carry

    lax.fori_loop(0, -(-n_steps // SUBLANES), group, 0)

    for p in range(N_PAIRS):
        y_ref[0, :, LANES * p:LANES * (p + 1)] = yt_ref[p].T

    @pl.when(c == pl.num_programs(1) - 1)
    def _():
        sT_ref[0] = s_ref[...]


def _rw_scan(r, w, k, v, kk, b, s0, n_valid):
    bsz, t, _ = r.shape
    tc = LANES
    assert t % tc == 0 and (n_valid == tc or t == tc)
    assert n_valid % SUBLANES == 0 or n_valid < SUBLANES
    seq = pl.BlockSpec((1, tc, D_MODEL), lambda bi, c: (bi, c, 0))
    st = pl.BlockSpec((1, N_PAIRS, HEAD_DIM, LANES), lambda bi, c: (bi, 0, 0, 0))
    return pl.pallas_call(
        functools.partial(_rw_scan_body, n_steps=n_valid),
        grid=(bsz, t // tc),
        in_specs=[seq] * 6 + [st],
        out_specs=[seq, st],
        out_shape=[jax.ShapeDtypeStruct((bsz, t, D_MODEL), F32),
                   jax.ShapeDtypeStruct((bsz, N_PAIRS, HEAD_DIM, LANES), F32)],
        scratch_shapes=[pltpu.VMEM((N_PAIRS, HEAD_DIM, LANES), F32),
                        pltpu.VMEM((N_PAIRS, LANES, tc), F32),
                        pltpu.VMEM((N_PAIRS, LANES, tc), F32)],
        compiler_params=_cparams("parallel", "arbitrary"),
        name="rw_scan",
    )(r, w, k, v, kk, b, s0)


CHUNK = 64


def _dot3(a, b, dims=_NN):
    ah, al = _split_bf16(a)
    bh, bl = _split_bf16(b)
    d = lambda x, y: lax.dot_general(x, y, dims, preferred_element_type=F32)
    return d(ah, bh) + d(ah, bl) + d(al, bh)


def _unit_lower_inverse(mats, n_head):
    n = mats[0].shape[0]
    r = lax.broadcasted_iota(jnp.int32, (n, n), 0)
    c = lax.broadcasted_iota(jnp.int32, (n, n), 1)
    same = lambda m: (r // m) == (c // m)
    eye = jnp.where(r == c, 1.0, 0.0)
    nb = [jnp.where(same(SUBLANES), a, 0.0) for a in mats]
    nb2 = [_dot3(x, x) for x in nb]
    inv = [eye - x for x in nb]
    inv = [t + _dot3(t, x2) for t, x2 in zip(inv, nb2)]
    nb4 = [_dot3(x2, x2) for x2 in nb2]
    inv = [t + _dot3(t, x4) for t, x4 in zip(inv, nb4)]
    m = SUBLANES
    while m < n_head:
        sel = same(2 * m) & jnp.logical_not(same(m))
        prod = [_dot3(jnp.where(sel, a, 0.0), t) for a, t in zip(mats, inv)]
        inv = [t - _dot3(t, x) for t, x in zip(inv, prod)]
        m *= 2
    return inv


def _rw_chunk_prep_body(r_ref, lw_ref, k_ref, v_ref, kk_ref, b_ref, tri_ref, w3_ref, w4_ref, g_ref, h_ref):
    cs = r_ref.shape[1]
    lw = lw_ref[0]
    hi, lo = _split_bf16(lw)
    tri = tri_ref[...]
    cum = jnp.dot(tri, hi, preferred_element_type=F32) + jnp.dot(tri, lo, preferred_element_type=F32)
    cum_end = cum[cs - 1:cs, :]
    grow = jnp.exp(-cum)
    to_end = jnp.exp(cum_end - cum)
    k = k_ref[0]
    b = b_ref[0]
    kt_all = k * grow
    bt_all = b * grow
    kd_all = k * to_end
    bd_all = b * to_end
    kkh_all = kk_ref[0] * jnp.exp(cum - lw)
    rh_all = r_ref[0] * jnp.exp(cum)
    dend_all = jnp.exp(cum_end)
    v_all = v_ref[0]

    n = 2 * cs
    lane = lax.broadcasted_iota(jnp.int32, (cs, LANES), 1)
    even = lane < HEAD_DIM
    row = lax.broadcasted_iota(jnp.int32, (n, n), 0)
    col = lax.broadcasted_iota(jnp.int32, (n, n), 1)
    same_head = (row // cs) == (col // cs)
    strict = same_head & ((col % cs) < (row % cs))
    incl = same_head & ((col % cs) <= (row % cs))
    diag = lax.broadcasted_iota(jnp.int32, (LANES, LANES), 0) == lax.broadcasted_iota(jnp.int32, (LANES, LANES), 1)
    pairs = range(N_PAIRS)
    sl = [slice(LANES * p, LANES * (p + 1)) for p in pairs]

    def embed(x):
        return [jnp.concatenate([jnp.where(even, x[:, s], 0.0), jnp.where(even, 0.0, x[:, s])], axis=0) for s in sl]

    kt, bt, kd, bd, kkh, rh, v = (embed(x) for x in (kt_all, bt_all, kd_all, bd_all, kkh_all, rh_all, v_all))
    a = [_dot3(jnp.concatenate([kkh[p], rh[p]], axis=0), jnp.concatenate([kt[p], bt[p]], axis=0), _NT)
         for p in pairs]
    a_kk = [jnp.where(strict, x[0:n, 0:n], 0.0) for x in a]
    a_kb = [jnp.where(strict, x[0:n, n:2 * n], 0.0) for x in a]
    a_rk = [jnp.where(incl, x[n:2 * n, 0:n], 0.0) for x in a]
    a_rb = [jnp.where(incl, x[n:2 * n, n:2 * n], 0.0) for x in a]
    pv = [_dot3(jnp.concatenate([a_kk[p], a_rk[p]], axis=0), v[p]) for p in pairs]
    hkv = [_dot3(kd[p], v[p], _TN) for p in pairs]
    inv = _unit_lower_inverse(a_kb, cs)
    w12 = [_dot3(inv[p], jnp.concatenate([kkh[p], pv[p][0:n]], axis=1)) for p in pairs]
    q12 = [_dot3(a_rb[p], w12[p]) for p in pairs]
    e12 = [_dot3(bd[p], w12[p], _TN) for p in pairs]
    for p in pairs:
        w3 = rh[p] - q12[p][:, 0:LANES]
        w4 = pv[p][n:2 * n] - q12[p][:, LANES:2 * LANES]
        w3_ref[0, :, sl[p]] = w3[0:cs] + w3[cs:n]
        w4_ref[0, :, sl[p]] = w4[0:cs] + w4[cs:n]
        g_ref[0, 0, p] = jnp.where(diag, dend_all[:, sl[p]], 0.0) - e12[p][:, 0:LANES]
        h_ref[0, 0, p] = hkv[p] - e12[p][:, LANES:2 * LANES]


def _rw_chunk_prep(r, lw, k, v, kk, b):
    bsz, t, _ = r.shape
    nc = t // CHUNK
    idx = jnp.arange(CHUNK)
    tri = (idx[None, :] <= idx[:, None]).astype(BF16)
    seq = pl.BlockSpec((1, CHUNK, D_MODEL), lambda bi, c: (bi, c, 0))
    mat = pl.BlockSpec((1, 1, N_PAIRS, LANES, LANES), lambda bi, c: (bi, c, 0, 0, 0))
    mat_shape = jax.ShapeDtypeStruct((bsz, nc, N_PAIRS, LANES, LANES), F32)
    return pl.pallas_call(
        _rw_chunk_prep_body,
        grid=(bsz, nc),
        in_specs=[seq] * 6 + [pl.BlockSpec((CHUNK, CHUNK), lambda bi, c: (0, 0))],
        out_specs=[seq, seq, mat, mat],
        out_shape=[jax.ShapeDtypeStruct((bsz, t, D_MODEL), F32)] * 2 + [mat_shape] * 2,
        compiler_params=_cparams("parallel", "parallel"),
        name="rw_chunk_prep",
    )(r, lw, k, v, kk, b, tri)


def _rw_chunk_scan_body(w3_ref, w4_ref, g_ref, h_ref, z0_ref, y_ref, zT_ref, z_ref):
    c = pl.program_id(1)
    cs = w3_ref.shape[1]

    @pl.when(c == 0)
    def _():
        z_ref[...] = z0_ref[0]

    for p in range(N_PAIRS):
        sl = slice(LANES * p, LANES * (p + 1))
        out = _dot3(jnp.concatenate([w3_ref[0, :, sl], g_ref[0, 0, p]], axis=0), z_ref[p])
        y_ref[0, :, sl] = out[0:cs] + w4_ref[0, :, sl]
        z_ref[p] = out[cs:cs + LANES] + h_ref[0, 0, p]

    @pl.when(c == pl.num_programs(1) - 1)
    def _():
        zT_ref[0] = z_ref[...]


def _rw_chunk_scan(w3, w4, g, h, z0):
    bsz, t, _ = w3.shape
    seq = pl.BlockSpec((1, CHUNK, D_MODEL), lambda bi, c: (bi, c, 0))
    mat = pl.BlockSpec((1, 1, N_PAIRS, LANES, LANES), lambda bi, c: (bi, c, 0, 0, 0))
    st = pl.BlockSpec((1, N_PAIRS, LANES, LANES), lambda bi, c: (bi, 0, 0, 0))
    return pl.pallas_call(
        _rw_chunk_scan_body,
        grid=(bsz, t // CHUNK),
        in_specs=[seq, seq, mat, mat, st],
        out_specs=[seq, st],
        out_shape=[jax.ShapeDtypeStruct((bsz, t, D_MODEL), F32),
                   jax.ShapeDtypeStruct((bsz, N_PAIRS, LANES, LANES), F32)],
        scratch_shapes=[pltpu.VMEM((N_PAIRS, LANES, LANES), F32)],
        compiler_params=_cparams("parallel", "arbitrary"),
        name="rw_chunk_scan",
    )(w3, w4, g, h, z0)


def _state_to_blockdiag(s):
    b = s.shape[0]
    st = jnp.swapaxes(s, 2, 3).reshape(b, N_PAIRS, 2, HEAD_DIM, HEAD_DIM)
    z = jnp.zeros((b, N_PAIRS, 2, HEAD_DIM, 2, HEAD_DIM), F32)
    z = z.at[:, :, 0, :, 0, :].set(st[:, :, 0]).at[:, :, 1, :, 1, :].set(st[:, :, 1])
    return z.reshape(b, N_PAIRS, LANES, LANES)


def _blockdiag_to_state(z):
    b = z.shape[0]
    z6 = z.reshape(b, N_PAIRS, 2, HEAD_DIM, 2, HEAD_DIM)
    st = jnp.stack([z6[:, :, 0, :, 0, :], z6[:, :, 1, :, 1, :]], axis=2)
    return jnp.swapaxes(st, 3, 4).reshape(b, N_HEADS, HEAD_DIM, HEAD_DIM)


def _rw_post_body(y_ref, bonus_ref, z_ref, g_ref, b_ref, seg_ref, o_ref):
    y = y_ref[...]
    mu = _seg_sum(y, seg_ref) * (1.0 / HEAD_DIM)
    d = y - mu
    var = _seg_sum(d * d, seg_ref) * (1.0 / HEAD_DIM)
    yn = d * lax.rsqrt(var + LNX_EPS) * g_ref[...] + b_ref[...]
    o_ref[...] = ((yn + bonus_ref[...]) * _silu(z_ref[...])).astype(BF16)


def _rw_post(y, bonus, u, g, b, seg):
    m = y.shape[0]
    tm = _row_tile(m, 512)
    row = pl.BlockSpec((tm, D_MODEL), lambda i: (i, 0))
    vec = pl.BlockSpec((1, D_MODEL), lambda i: (0, 0))
    return pl.pallas_call(
        _rw_post_body,
        grid=(m // tm,),
        in_specs=[row, row, pl.BlockSpec((tm, D_MODEL), lambda i: (i, C_RZ)), vec, vec,
                  pl.BlockSpec((D_MODEL, D_MODEL), lambda i: (0, 0))],
        out_specs=row,
        out_shape=jax.ShapeDtypeStruct((m, D_MODEL), BF16),
        compiler_params=_cparams("parallel"),
        name="rw_post",
    )(y, bonus, u, g, b, seg)


def _merge_body(x_ref, oa_ref, ob_ref, oc_ref, g0_ref, g1_ref, g2_ref,
                wa_ref, wb_ref, wc_ref, wo_ref, o_ref):
    merged = (_sigmoid(g0_ref[...]) * jnp.dot(oa_ref[...], wa_ref[...], preferred_element_type=F32)
              + _sigmoid(g1_ref[...]) * jnp.dot(ob_ref[...], wb_ref[...], preferred_element_type=F32)
              + _sigmoid(g2_ref[...]) * jnp.dot(oc_ref[...], wc_ref[...], preferred_element_type=F32))
    o_ref[...] = x_ref[...] + jnp.dot(merged.astype(BF16), wo_ref[...], preferred_element_type=F32)


def _merge(x, oa, ob, oc, u, wa, wb, wc, wo):
    m = x.shape[0]
    tm = _row_tile(m, 512)
    row = pl.BlockSpec((tm, D_MODEL), lambda i: (i, 0))
    col = lambda c: pl.BlockSpec((tm, D_MODEL), lambda i, c=c: (i, c))
    wsp = pl.BlockSpec((D_MODEL, D_MODEL), lambda i: (0, 0))
    return pl.pallas_call(
        _merge_body,
        grid=(m // tm,),
        in_specs=[row, row, row, row, col(C_G0), col(C_G1), col(C_G2), wsp, wsp, wsp, wsp],
        out_specs=row,
        out_shape=jax.ShapeDtypeStruct((m, D_MODEL), F32),
        compiler_params=_cparams("parallel"),
        name="merge",
    )(x, oa, ob, oc, u, u, u, wa, wb, wc, wo)


def _later_matrix(n, lhs):
    i = jnp.arange(n)
    m = (i[:, None] > i[None, :]) if not lhs else (i[None, :] > i[:, None])
    return m.astype(BF16)


def _pack_pairs(s):
    b = s.shape[0]
    return s.reshape(b, N_PAIRS, 2, HEAD_DIM, HEAD_DIM).transpose(0, 1, 3, 2, 4).reshape(b, N_PAIRS, HEAD_DIM, LANES)


def _unpack_pairs(s):
    b = s.shape[0]
    return s.reshape(b, N_PAIRS, HEAD_DIM, 2, HEAD_DIM).transpose(0, 1, 3, 2, 4).reshape(b, N_HEADS, HEAD_DIM, HEAD_DIM)


def _layer(x, conv_prev, shift_prev, wkv_prev, lw, attend):
    b, t, _ = x.shape
    m = b * t
    x2 = x.reshape(m, D_MODEL)
    u = _in_proj(x2, lw['norm_g'], lw['w_main'], 1024)
    ul = _in_proj(x2, lw['norm_g'], lw['w_lora'], LANES)

    glu, qb, kf, kb, vb = _pre(u, lw['q_g'], lw['k_g'], lw['seg'])
    u3 = u.reshape(b, t, N_MAIN)

    glu3 = glu.reshape(b, t, D_MODEL)
    hist = jnp.concatenate([jnp.zeros((b, CONV_HALO - (CONV_W - 1), D_MODEL), F32), conv_prev], axis=1)
    ext = jnp.concatenate([hist, glu3], axis=1)
    oa = _conv(glu3, ext, u3, lw['conv_w'], lw['conv_b'], lw['conv_ln_g'], lw['conv_ln_b'])
    conv_new = ext[:, -(CONV_W - 1):]

    ob = attend(qb.reshape(b, t, D_MODEL), kb.reshape(b, t, D_MODEL), vb.reshape(b, t, D_MODEL), u3)
    v_out = u3[:, :, C_V * D_MODEL:(C_V + 1) * D_MODEL]

    p_main = u3[:, :, C_R * D_MODEL:(C_RV + 1) * D_MODEL]
    ul3 = ul.reshape(b, t, LANES)
    prev = jnp.concatenate([shift_prev[:, None, :3 * D_MODEL], p_main[:, :-1]], axis=1).reshape(m, 3 * D_MODEL)
    prev_l = jnp.concatenate([shift_prev[:, None, 3 * D_MODEL:], ul3[:, :-1]], axis=1).reshape(m, LANES)
    r, w, k, v, kk, bb, bonus = _rw_prep(u, ul, prev, prev_l, lw['mu3'], lw['mul'], lw['w0'], lw['w2p'],
                                         lw['a0'], lw['a2p'], lw['k_k'], lw['k_a'], lw['r_k'], lw['seg'])
    seqs = [a.reshape(b, t, D_MODEL) for a in (r, w, k, v, kk, bb)]
    if t % CHUNK == 0:
        w3, w4, g, h = _rw_chunk_prep(*seqs)
        y, z_new = _rw_chunk_scan(w3, w4, g, h, _state_to_blockdiag(wkv_prev))
        s_new = _blockdiag_to_state(z_new)
    else:
        assert t < SUBLANES
        seqs = [jnp.pad(a, ((0, 0), (0, LANES - t), (0, 0))) for a in seqs]
        y, s_new = _rw_scan(*seqs, _pack_pairs(wkv_prev), t)
        y, s_new = y[:, :t], _unpack_pairs(s_new)
    y = y.reshape(m, D_MODEL)
    oc = _rw_post(y, bonus, u, lw['lnx_g'], lw['lnx_b'], lw['seg'])
    shift_new = jnp.concatenate([p_main[:, -1], ul3[:, -1]], axis=-1)

    x_new = _merge(x2, oa.reshape(m, D_MODEL), ob.reshape(m, D_MODEL), oc, u,
                   lw['w_pa'], lw['w_pb'], lw['w_pc'], lw['w_out'])
    return (x_new.reshape(b, t, D_MODEL), kf.reshape(b, t, N_HEADS, HEAD_DIM),
            v_out.reshape(b, t, N_HEADS, HEAD_DIM), s_new, shift_new, conv_new)


def kernel(x_prompt, x_sample, cache_k, cache_v, state_wkv, state_shift, state_conv, page_table, norm_g, w_in,
           conv_w, conv_b, conv_ln_g, conv_ln_b, w_pa, q_norm_g, k_norm_g, sb_bias, w_pb, mu_shift, w0, w2, a0,
           a2, k_k, k_a, r_k, lnx_g, lnx_b, w_pc, w_out):
    depth = w_in.shape[0]
    nb_p, t_p, _ = x_prompt.shape
    nb_s, t_s, _ = x_sample.shape
    n_pool = cache_k.shape[0]
    lora0 = 10 * D_MODEL
    seg = jnp.kron(jnp.eye(N_HEADS, dtype=F32), jnp.ones((HEAD_DIM, HEAD_DIM), F32)).astype(BF16)
    vec = lambda a: a.reshape(1, -1)
    cache_kt = jnp.transpose(cache_k, (0, 1, 3, 4, 2)).reshape(n_pool, depth, D_MODEL, PAGE)
    cache_vt = jnp.transpose(cache_v, (0, 1, 3, 4, 2)).reshape(n_pool, depth, D_MODEL, PAGE)
    assert t_s <= PAGE
    tri_prompt = _later_matrix(SB_TK, lhs=False)
    tri_page = jnp.tile(_later_matrix(PAGE, lhs=False), (2, 1))
    qrow_head = jnp.arange(N_HEADS * t_s) // t_s
    lane_head = jnp.arange(D_MODEL) // HEAD_DIM

    x_p, x_s = x_prompt, x_sample
    outs_p = [[] for _ in range(5)]
    outs_s = [[] for _ in range(5)]
    for l in range(depth):
        wl = w_in[l]
        zpad = jnp.zeros((LORA, D_MODEL), F32)
        lw = {
            'norm_g': vec(norm_g[l]),
            'w_main': jnp.concatenate([wl[:, :lora0], wl[:, lora0 + 2 * LORA:]], axis=1).astype(BF16),
            'w_lora': wl[:, lora0:lora0 + 2 * LORA].astype(BF16),
            'q_g': vec(jnp.tile(q_norm_g[l], N_HEADS)), 'k_g': vec(jnp.tile(k_norm_g[l], N_HEADS)),
            'seg': seg,
            'conv_w': jnp.concatenate([conv_w[l], jnp.zeros((CONV_HALO - CONV_W, D_MODEL), F32)], axis=0),
            'conv_b': vec(conv_b[l]), 'conv_ln_g': vec(conv_ln_g[l]), 'conv_ln_b': vec(conv_ln_b[l]),
            'mu3': mu_shift[l, :3 * D_MODEL].reshape(3, D_MODEL), 'mul': vec(mu_shift[l, 3 * D_MODEL:]),
            'w0': vec(w0[l]), 'w2p': jnp.concatenate([w2[l], zpad], axis=0).astype(BF16),
            'a0': vec(a0[l]), 'a2p': jnp.concatenate([zpad, a2[l]], axis=0).astype(BF16),
            'k_k': vec(k_k[l]), 'k_a': vec(k_a[l]), 'r_k': vec(r_k[l]),
            'lnx_g': vec(lnx_g[l]), 'lnx_b': vec(lnx_b[l]),
            'w_pa': w_pa[l].astype(BF16), 'w_pb': w_pb[l].astype(BF16), 'w_pc': w_pc[l].astype(BF16),
            'w_out': w_out[l].astype(BF16),
        }
        bias_l = sb_bias[l]

        def attend_prompt(qb, kb, vb, u3):
            return _sb_prompt(qb, kb, vb, u3, bias_l, tri_prompt, SB_TQ, SB_TK, SB_SPLIT)

        def attend_sample(qb, kb, vb, u3):
            qt = jnp.tile(qb, (1, N_HEADS, 1))
            qt = jnp.where(qrow_head[:, None] == lane_head[None, :], qt, jnp.zeros_like(qt))
            pad = ((0, 0), (0, PAGE - t_s), (0, 0))
            sbz = u3[:, :, C_SBZ * D_MODEL:(C_SBZ + 1) * D_MODEL]
            return _sb_sample(page_table, qt, jnp.repeat(bias_l, t_s)[:, None], jnp.pad(kb, pad), jnp.pad(vb, pad),
                              cache_kt, cache_vt, l, sbz, tri_page)

        res_p = _layer(x_p, jnp.zeros((nb_p, CONV_W - 1, D_MODEL), F32), jnp.zeros((nb_p, state_shift.shape[2]), F32),
                       jnp.zeros((nb_p,) + state_wkv.shape[2:], F32), lw, attend_prompt)
        res_s = _layer(x_s, state_conv[:, l], state_shift[:, l], state_wkv[:, l], lw, attend_sample)
        x_p, x_s = res_p[0], res_s[0]
        for i in range(5):
            outs_p[i].append(res_p[i + 1])
            outs_s[i].append(res_s[i + 1])
    stack = lambda xs: jnp.stack(xs, axis=1)
    return (x_p, x_s) + tuple(stack(o) for o in outs_p) + tuple(stack(o) for o in outs_s)
```
